```python
import math
import jax, jax.numpy as jnp
from jax import lax
import numpy as np

D_MODEL = 1024
BATCH = 4
SEQ = 4096
DEPTH = 1
DEC_BATCH = 128
DEC_SEQ = 8
PAST_LEN = 8192
PAGE_SIZE = 128

A_HEADS = 8
A_HEAD_DIM = 64
A_WIDTH = A_HEADS * A_HEAD_DIM
MOBA_BLOCK = 256
MOBA_TOPK = 3
B_HEADS = 4
B_HEAD_DIM = 64
B_V_DIM = 2 * B_HEAD_DIM
B_QK_WIDTH = B_HEADS * 2 * B_HEAD_DIM
B_V_WIDTH = B_HEADS * B_V_DIM
N_GROUPS = 4
EXPERTS_PER_GROUP = 4
N_EXPERTS = N_GROUPS * EXPERTS_PER_GROUP
EXPERT_TOPK = 2
D_EXPERT = D_MODEL // 4
SPLITS = (A_WIDTH, A_WIDTH, A_WIDTH, B_QK_WIDTH, B_QK_WIDTH, B_V_WIDTH, D_MODEL, D_MODEL)
IN_WIDTH = sum(SPLITS)
V_COLUMN_SLOTS = (2, 5)
Q_BLOCK = 128
DN_ALPHA = (2 * DEPTH) ** 0.25
DN_BETA = (8 * DEPTH) ** -0.25
LN_EPS = 1e-5

kernel_name = 'gated_moba_diffattn_hmoe_decoder_step'


def lambda_init(layer):
    return 0.8 - 0.6 * math.exp(-0.3 * layer)


def alibi_slopes(n_heads):
    return jnp.asarray(np.power(2.0, -8.0 * np.arange(1, n_heads + 1) / n_heads), dtype=jnp.float32)


def layer_norm(x, g, b):
    xf = x.astype(jnp.float32)
    mu = jnp.mean(xf, -1, keepdims=True)
    var = jnp.mean(jnp.square(xf - mu), -1, keepdims=True)
    return ((xf - mu) * lax.rsqrt(var + LN_EPS) * g + b).astype(x.dtype)


def project_inputs(x, w_in, b_gate):
    lead = x.shape[:-1]
    z = jnp.einsum('...d,de->...e', x, w_in)
    offs = [int(o) for o in np.cumsum(SPLITS)[:-1]]
    qa, ka, va, qb, kb, vb, ga, gb = jnp.split(z, offs, axis=-1)
    qa = qa.reshape(lead + (A_HEADS, A_HEAD_DIM))
    ka = ka.reshape(lead + (A_HEADS, A_HEAD_DIM))
    va = va.reshape(lead + (A_HEADS, A_HEAD_DIM))
    qb = qb.reshape(lead + (B_HEADS, 2, B_HEAD_DIM))
    kb = kb.reshape(lead + (B_HEADS, 2 * B_HEAD_DIM))
    vb = vb.reshape(lead + (B_HEADS, B_V_DIM))
    g_a = jax.nn.sigmoid(ga + b_gate[:D_MODEL])
    g_b = jax.nn.sigmoid(gb + b_gate[D_MODEL:])
    return qa, ka, va, qb, kb, vb, g_a, g_b


def moba_blocks(k, v):
    length = k.shape[0]
    nb = -(-length // MOBA_BLOCK)
    pad = nb * MOBA_BLOCK - length
    def to_blocks(t):
        t = jnp.pad(t, ((0, pad), (0, 0), (0, 0)))
        return t.reshape(nb, MOBA_BLOCK, t.shape[1], t.shape[2]).transpose(2, 0, 1, 3)
    kbh, vbh = to_blocks(k), to_blocks(v)
    k_mean = jnp.mean(kbh.astype(jnp.float32), axis=2).astype(k.dtype)
    return kbh, vbh, k_mean


def moba_attend(q, q_pos, kbh, vbh, k_mean, slopes):
    n_heads, nb = kbh.shape[0], kbh.shape[1]
    n_q = q.shape[0]
    n_sel = min(MOBA_TOPK, nb)
    q_blk = q_pos // MOBA_BLOCK
    gate = jnp.einsum('qhd,hnd->hqn', q, k_mean).astype(jnp.float32)
    fully_past = jnp.arange(nb)[None, None, :] < q_blk[None, :, None]
    gate = jnp.where(fully_past, gate, -jnp.inf)
    _, top_idx = lax.top_k(gate, n_sel)
    sel_ok = top_idx < q_blk[None, :, None]
    own = jnp.broadcast_to(q_blk[None, :, None], (n_heads, n_q, 1)).astype(top_idx.dtype)
    blk_idx = jnp.concatenate([top_idx, own], axis=-1)
    ok = jnp.concatenate([sel_ok, jnp.ones((n_heads, n_q, 1), bool)], axis=-1)
    h_idx = jnp.arange(n_heads)[:, None, None]
    k_g = kbh[h_idx, blk_idx]
    v_g = vbh[h_idx, blk_idx]
    k_pos = blk_idx[..., None] * MOBA_BLOCK + jnp.arange(MOBA_BLOCK)
    is_own = (jnp.arange(n_sel + 1) == n_sel)[None, None, :, None]
    valid = jnp.where(is_own, k_pos <= q_pos[None, :, None, None], ok[..., None])
    dist = (q_pos[None, :, None, None] - k_pos).astype(jnp.float32)
    s = jnp.einsum('qhd,hqjkd->hqjk', q, k_g).astype(jnp.float32) * (A_HEAD_DIM ** -0.5)
    s = s - slopes[:, None, None, None] * dist
    s = jnp.where(valid, s, -jnp.inf)
    n_keys = (n_sel + 1) * MOBA_BLOCK
    p = jax.nn.softmax(s.reshape(n_heads, n_q, n_keys), axis=-1).astype(v_g.dtype)
    return jnp.einsum('hqm,hqmd->qhd', p, v_g.reshape(n_heads, n_q, n_keys, A_HEAD_DIM))


def diff_attend(q, q_pos, k, v, k_pos, lam, slopes, subln_w, lam_init):
    length, n_heads = k.shape[0], k.shape[1]
    k = k.reshape(length, n_heads, 2, B_HEAD_DIM)
    s = jnp.einsum('qhmd,khmd->hmqk', q, k).astype(jnp.float32) * (B_HEAD_DIM ** -0.5)
    dist = (q_pos[:, None] - k_pos[None, :]).astype(jnp.float32)
    s = s - slopes[:, None, None, None] * dist
    s = jnp.where(k_pos[None, :] <= q_pos[:, None], s, -jnp.inf)
    p = jax.nn.softmax(s, axis=-1)
    a = (p[:, 0] - lam * p[:, 1]).astype(v.dtype)
    o = jnp.einsum('hqk,khe->qhe', a, v).astype(jnp.float32)
    o = o * lax.rsqrt(jnp.mean(o * o, -1, keepdims=True) + LN_EPS) * subln_w * (1.0 - lam_init)
    return o.astype(v.dtype)


def prompt_attention(qa, ka, va, qb, kb, vb, lam, subln_w, lam_init, slopes_a, slopes_b):
    bsz, seq = qa.shape[0], qa.shape[1]
    qc = min(Q_BLOCK, seq)
    nc = seq // qc
    pos = jnp.arange(seq, dtype=jnp.int32)
    kbh, vbh, kmean = jax.vmap(moba_blocks)(ka, va)
    q_chunks = qa.reshape(bsz * nc, qc, A_HEADS, A_HEAD_DIM)
    pos_chunks = jnp.tile(pos.reshape(nc, qc), (bsz, 1))
    b_idx = jnp.repeat(jnp.arange(bsz, dtype=jnp.int32), nc)
    def moba_chunk(args):
        qch, pch, b = args
        return moba_attend(qch, pch, kbh[b], vbh[b], kmean[b], slopes_a)
    o_a = lax.map(moba_chunk, (q_chunks, pos_chunks, b_idx)).reshape(bsz, seq, A_WIDTH)
    qb_chunks = qb.reshape(bsz, nc, qc, B_HEADS, 2, B_HEAD_DIM).swapaxes(0, 1)
    def diff_chunk(args):
        qch, pch = args
        one = lambda q_, k_, v_: diff_attend(q_, pch, k_, v_, pos, lam, slopes_b, subln_w, lam_init)
        return jax.vmap(one)(qch, kb, vb)
    o_b = lax.map(diff_chunk, (qb_chunks, pos.reshape(nc, qc)))
    o_b = o_b.swapaxes(0, 1).reshape(bsz, seq, B_V_WIDTH)
    return o_a, o_b


def sample_attention(qa, ka, va, qb, kb, vb, pool_ak, pool_av, pool_bk, pool_bv, page_table,
                     lam, subln_w, lam_init, slopes_a, slopes_b):
    n_seq, n_new = qa.shape[0], qa.shape[1]
    past_len = page_table.shape[1] * pool_ak.shape[1]
    q_pos = past_len + jnp.arange(n_new, dtype=jnp.int32)
    k_pos = jnp.arange(past_len + n_new, dtype=jnp.int32)
    def one_seq(args):
        qa_s, ka_s, va_s, qb_s, kb_s, vb_s, pages = args
        def with_past(pool, new):
            past = pool[pages].reshape((past_len,) + pool.shape[2:])
            return jnp.concatenate([past, new], axis=0)
        kbh, vbh, kmean = moba_blocks(with_past(pool_ak, ka_s), with_past(pool_av, va_s))
        o_a = moba_attend(qa_s, q_pos, kbh, vbh, kmean, slopes_a)
        o_b = diff_attend(qb_s, q_pos, with_past(pool_bk, kb_s), with_past(pool_bv, vb_s), k_pos,
                          lam, slopes_b, subln_w, lam_init)
        return o_a, o_b
    o_a, o_b = lax.map(one_seq, (qa, ka, va, qb, kb, vb, page_table))
    return o_a.reshape(n_seq, n_new, A_WIDTH), o_b.reshape(n_seq, n_new, B_V_WIDTH)


def gated_merge(o_a, o_b, g_a, g_b, w_proj_a, w_proj_b, w_out):
    y = g_a * (o_a @ w_proj_a) + g_b * (o_b @ w_proj_b)
    return y @ w_out


def hier_moe(x, w_rg, b_rg, w_re, b_re, w_eg, w_eu, w_ed):
    lead = x.shape[:-1]
    xf = x.reshape(-1, D_MODEL)
    lg = (xf @ w_rg + b_rg).astype(jnp.float32)
    g_sel = jnp.argmax(lg, axis=-1)
    p_grp = jnp.take_along_axis(jax.nn.softmax(lg, axis=-1), g_sel[:, None], axis=1)
    le = (xf @ w_re + b_re).astype(jnp.float32).reshape(-1, N_GROUPS, EXPERTS_PER_GROUP)
    le = jnp.take_along_axis(le, g_sel[:, None, None], axis=1)[:, 0]
    top_v, top_i = lax.top_k(le, EXPERT_TOPK)
    w_sel = jax.nn.softmax(top_v, axis=-1) * p_grp
    e_sel = g_sel[:, None] * EXPERTS_PER_GROUP + top_i
    combine = jnp.sum(jax.nn.one_hot(e_sel, N_EXPERTS, dtype=jnp.float32) * w_sel[..., None], axis=1)
    h = jax.nn.silu(jnp.einsum('td,edf->tef', xf, w_eg)) * jnp.einsum('td,edf->tef', xf, w_eu)
    h = h * combine[..., None].astype(h.dtype)
    y = jnp.einsum('tef,efd->td', h, w_ed)
    return y.reshape(lead + (D_MODEL,))


def decoder_layer(h, attend, w_in, b_gate, w_proj_a, w_proj_b, w_out, ln1_g, ln1_b, ln2_g, ln2_b,
                  w_rg, b_rg, w_re, b_re, w_eg, w_eu, w_ed):
    qa, ka, va, qb, kb, vb, g_a, g_b = project_inputs(h, w_in, b_gate)
    o_a, o_b = attend(qa, ka, va, qb, kb, vb)
    mix = gated_merge(o_a, o_b, g_a, g_b, w_proj_a, w_proj_b, w_out)
    h = layer_norm(DN_ALPHA * h + mix, ln1_g, ln1_b)
    h = layer_norm(DN_ALPHA * h + hier_moe(h, w_rg, b_rg, w_re, b_re, w_eg, w_eu, w_ed), ln2_g, ln2_b)
    return h, (ka, va, kb, vb)


def setup_inputs(seed: int = 0) -> dict:
    key = jax.random.key(seed)
    ks = jax.random.split(key, 32)
    f32 = jnp.float32
    n_pages = PAST_LEN // PAGE_SIZE
    n_pool = (DEC_BATCH * n_pages * 5) // 4
    def nrm(k, shape, scale):
        return jax.random.normal(k, shape, f32) * scale
    col_scale = jnp.asarray(np.concatenate([np.full((s,), DN_BETA if i in V_COLUMN_SLOTS else 1.0, np.float32)
                                            for i, s in enumerate(SPLITS)]))
    x_prompt = nrm(ks[0], (BATCH, SEQ, D_MODEL), 1.0)
    x_sample = nrm(ks[1], (DEC_BATCH, DEC_SEQ, D_MODEL), 1.0)
    cache_moba_k = nrm(ks[2], (DEPTH, n_pool, PAGE_SIZE, A_HEADS, A_HEAD_DIM), 1.0)
    cache_moba_v = nrm(ks[3], (DEPTH, n_pool, PAGE_SIZE, A_HEADS, A_HEAD_DIM), DN_BETA)
    cache_diff_k = nrm(ks[4], (DEPTH, n_pool, PAGE_SIZE, B_HEADS, 2 * B_HEAD_DIM), 1.0)
    cache_diff_v = nrm(ks[5], (DEPTH, n_pool, PAGE_SIZE, B_HEADS, B_V_DIM), DN_BETA)
    page_table = jax.random.permutation(ks[6], n_pool)[:DEC_BATCH * n_pages].reshape(DEC_BATCH, n_pages).astype(jnp.int32)
    return {
        'x_prompt': x_prompt,
        'x_sample': x_sample,
        'cache_moba_k': cache_moba_k,
        'cache_moba_v': cache_moba_v,
        'cache_diff_k': cache_diff_k,
        'cache_diff_v': cache_diff_v,
        'page_table': page_table,
        'w_in': nrm(ks[7], (DEPTH, D_MODEL, IN_WIDTH), D_MODEL ** -0.5) * col_scale,
        'b_gate': nrm(ks[8], (DEPTH, 2 * D_MODEL), 0.02),
        'lambda_q1': nrm(ks[9], (DEPTH, B_HEAD_DIM), 0.1),
        'lambda_k1': nrm(ks[10], (DEPTH, B_HEAD_DIM), 0.1),
        'lambda_q2': nrm(ks[11], (DEPTH, B_HEAD_DIM), 0.1),
        'lambda_k2': nrm(ks[12], (DEPTH, B_HEAD_DIM), 0.1),
        'subln_w': 1.0 + nrm(ks[13], (DEPTH, B_V_DIM), 0.02),
        'w_proj_a': nrm(ks[14], (DEPTH, A_WIDTH, D_MODEL), A_WIDTH ** -0.5 * DN_BETA),
        'w_proj_b': nrm(ks[15], (DEPTH, B_V_WIDTH, D_MODEL), B_V_WIDTH ** -0.5 * DN_BETA),
        'w_out': nrm(ks[16], (DEPTH, D_MODEL, D_MODEL), D_MODEL ** -0.5 * DN_BETA),
        'ln1_g': 1.0 + nrm(ks[17], (DEPTH, D_MODEL), 0.02),
        'ln1_b': nrm(ks[18], (DEPTH, D_MODEL), 0.02),
        'ln2_g': 1.0 + nrm(ks[19], (DEPTH, D_MODEL), 0.02),
        'ln2_b': nrm(ks[20], (DEPTH, D_MODEL), 0.02),
        'w_route_group': nrm(ks[21], (DEPTH, D_MODEL, N_GROUPS), D_MODEL ** -0.5),
        'b_route_group': nrm(ks[22], (DEPTH, N_GROUPS), 0.01),
        'w_route_expert': nrm(ks[23], (DEPTH, D_MODEL, N_EXPERTS), D_MODEL ** -0.5),
        'b_route_expert': nrm(ks[24], (DEPTH, N_EXPERTS), 0.01),
        'w_expert_gate': nrm(ks[25], (DEPTH, N_EXPERTS, D_MODEL, D_EXPERT), D_MODEL ** -0.5 * DN_BETA),
        'w_expert_up': nrm(ks[26], (DEPTH, N_EXPERTS, D_MODEL, D_EXPERT), D_MODEL ** -0.5 * DN_BETA),
        'w_expert_down': nrm(ks[27], (DEPTH, N_EXPERTS, D_EXPERT, D_MODEL), D_EXPERT ** -0.5 * DN_BETA),
    }


def reference(x_prompt, x_sample, cache_moba_k, cache_moba_v, cache_diff_k, cache_diff_v, page_table,
              w_in, b_gate, lambda_q1, lambda_k1, lambda_q2, lambda_k2, subln_w, w_proj_a, w_proj_b, w_out,
              ln1_g, ln1_b, ln2_g, ln2_b, w_route_group, b_route_group, w_route_expert, b_route_expert,
              w_expert_gate, w_expert_up, w_expert_down):
    slopes_a = alibi_slopes(A_HEADS)
    slopes_b = alibi_slopes(B_HEADS)
    h_p, h_s = x_prompt, x_sample
    rows_p, rows_s = [], []
    for l in range(DEPTH):
        lam_init = lambda_init(l)
        lam = (jnp.exp(jnp.sum(lambda_q1[l].astype(jnp.float32) * lambda_k1[l].astype(jnp.float32)))
               - jnp.exp(jnp.sum(lambda_q2[l].astype(jnp.float32) * lambda_k2[l].astype(jnp.float32)))
               + lam_init)
        layer_w = (w_in[l], b_gate[l], w_proj_a[l], w_proj_b[l], w_out[l], ln1_g[l], ln1_b[l], ln2_g[l], ln2_b[l],
                   w_route_group[l], b_route_group[l], w_route_expert[l], b_route_expert[l],
                   w_expert_gate[l], w_expert_up[l], w_expert_down[l])
        def attend_p(qa, ka, va, qb, kb, vb, lam=lam, lam_init=lam_init, l=l):
            return prompt_attention(qa, ka, va, qb, kb, vb, lam, subln_w[l], lam_init, slopes_a, slopes_b)
        def attend_s(qa, ka, va, qb, kb, vb, lam=lam, lam_init=lam_init, l=l):
            return sample_attention(qa, ka, va, qb, kb, vb, cache_moba_k[l], cache_moba_v[l],
                                    cache_diff_k[l], cache_diff_v[l], page_table,
                                    lam, subln_w[l], lam_init, slopes_a, slopes_b)
        h_p, r_p = decoder_layer(h_p, attend_p, *layer_w)
        h_s, r_s = decoder_layer(h_s, attend_s, *layer_w)
        rows_p.append(r_p)
        rows_s.append(r_s)
    new_moba_k_prompt = jnp.stack([r[0] for r in rows_p])
    new_moba_v_prompt = jnp.stack([r[1] for r in rows_p])
    new_diff_k_prompt = jnp.stack([r[2] for r in rows_p])
    new_diff_v_prompt = jnp.stack([r[3] for r in rows_p])
    new_moba_k_sample = jnp.stack([r[0] for r in rows_s])
    new_moba_v_sample = jnp.stack([r[1] for r in rows_s])
    new_diff_k_sample = jnp.stack([r[2] for r in rows_s])
    new_diff_v_sample = jnp.stack([r[3] for r in rows_s])
    return (h_p, h_s, new_moba_k_prompt, new_moba_v_prompt, new_diff_k_prompt, new_diff_v_prompt,
            new_moba_k_sample, new_moba_v_sample, new_diff_k_sample, new_diff_v_sample)
```

```python
import functools
import math

import numpy as np
import jax
import jax.numpy as jnp
from jax import lax
from jax.experimental import pallas as pl
from jax.experimental.pallas import tpu as pltpu

F32 = jnp.float32
BF16 = jnp.bfloat16

A_HEADS = 8
A_HEAD_DIM = 64
B_HEADS = 4
B_HEAD_DIM = 64
B_V_DIM = 2 * B_HEAD_DIM
WIDTH = A_HEADS * A_HEAD_DIM
MOBA_BLOCK = 256
MOBA_TOPK = 3
N_GROUPS = 4
EXPERTS_PER_GROUP = 4
N_EXPERTS = N_GROUPS * EXPERTS_PER_GROUP
LN_EPS = 1e-5
LANES = 128
HEAD_LANES = 64
GROUP_LANES = 256
NEG_INF = float("-inf")
VMEM_LIMIT = 56 * 1024 * 1024

_NT = (((1,), (1,)), ((), ()))


def _lambda_init(layer):
    return 0.8 - 0.6 * math.exp(-0.3 * layer)


def _alibi_slopes(n_heads):
    return jnp.asarray(np.power(2.0, -8.0 * np.arange(1, n_heads + 1) / n_heads), dtype=F32)


def _smem_spec():
    return pl.BlockSpec(memory_space=pltpu.SMEM)


def _layer_norm(x, g, b):
    mu = jnp.mean(x, axis=-1, keepdims=True)
    xc = x - mu
    var = jnp.mean(xc * xc, axis=-1, keepdims=True)
    return xc * lax.rsqrt(var + LN_EPS) * g + b


def _inproj_kernel(x_ref, w_ref, bg_ref, *refs, prompt):
    if prompt:
        (wvta_ref, wvtb_ref, qa_ref, ka_ref, va_ref, qb_ref, kb_ref, vb_ref, g_ref,
         kab_ref, kbb_ref, vat_ref, vbt_ref, kmean_ref) = refs
    else:
        qa_ref, ka_ref, va_ref, qb_ref, kb_ref, vb_ref, g_ref = refs
    xb = x_ref[...].astype(BF16)

    def proj(slot):
        return jnp.dot(xb, w_ref[:, slot * WIDTH:(slot + 1) * WIDTH], preferred_element_type=F32)

    q_scale = A_HEAD_DIM ** -0.5
    qa_ref[...] = (proj(0) * q_scale).astype(qa_ref.dtype)
    ka = proj(1)
    ka_ref[...] = ka
    va_ref[...] = proj(2)
    qb_ref[...] = (proj(3) * q_scale).astype(qb_ref.dtype)
    kb = proj(4)
    kb_ref[...] = kb
    vb_ref[...] = proj(5)
    z = jnp.dot(xb, w_ref[:, 6 * WIDTH:], preferred_element_type=F32) + bg_ref[...]
    g_ref[...] = 1.0 / (1.0 + jnp.exp(-z))
    if prompt:
        kab_ref[...] = ka.astype(BF16)
        kbb_ref[...] = kb.astype(BF16)
        kmean_ref[...] = jnp.mean(ka, axis=0, keepdims=True)
        vat_ref[...] = lax.dot_general(wvta_ref[...], xb, _NT, preferred_element_type=F32).astype(BF16)
        vbt_ref[...] = lax.dot_general(wvtb_ref[...], xb, _NT, preferred_element_type=F32).astype(BF16)


def _in_projection(x, w_bf, b_gate, wvta, wvtb, *, prompt):
    t, d = x.shape
    tm = MOBA_BLOCK
    assert t % tm == 0
    n_in = w_bf.shape[1]
    row = lambda i: (i, 0)
    const = lambda i: (0, 0)
    in_specs = [pl.BlockSpec((tm, d), row), pl.BlockSpec((d, n_in), const), pl.BlockSpec((1, 2 * d), const)]
    args = [x, w_bf, b_gate.reshape(1, 2 * d)]
    q_dtype = BF16 if prompt else F32
    out_shape = [jax.ShapeDtypeStruct((t, WIDTH), q_dtype), jax.ShapeDtypeStruct((t, WIDTH), F32),
                 jax.ShapeDtypeStruct((t, WIDTH), F32), jax.ShapeDtypeStruct((t, WIDTH), q_dtype),
                 jax.ShapeDtypeStruct((t, WIDTH), F32), jax.ShapeDtypeStruct((t, WIDTH), F32),
                 jax.ShapeDtypeStruct((t, 2 * d), F32)]
    out_specs = [pl.BlockSpec((tm, WIDTH), row)] * 6 + [pl.BlockSpec((tm, 2 * d), row)]
    if prompt:
        in_specs += [pl.BlockSpec((WIDTH, d), const)] * 2
        args += [wvta, wvtb]
        out_shape += [jax.ShapeDtypeStruct((t, WIDTH), BF16), jax.ShapeDtypeStruct((t, WIDTH), BF16),
                      jax.ShapeDtypeStruct((WIDTH, t), BF16), jax.ShapeDtypeStruct((WIDTH, t), BF16),
                      jax.ShapeDtypeStruct((t // tm, 1, WIDTH), F32)]
        out_specs += [pl.BlockSpec((tm, WIDTH), row)] * 2 + [pl.BlockSpec((WIDTH, tm), lambda i: (0, i))] * 2
        out_specs += [pl.BlockSpec((None, 1, WIDTH), lambda i: (i, 0, 0))]
    return pl.pallas_call(
        functools.partial(_inproj_kernel, prompt=prompt),
        grid=(t // tm,),
        in_specs=in_specs, out_specs=out_specs, out_shape=out_shape,
        compiler_params=pltpu.CompilerParams(dimension_semantics=("arbitrary",), vmem_limit_bytes=VMEM_LIMIT),
        name="in_projection_prompt" if prompt else "in_projection_sample",
    )(*args)


def _flash_step(j, carry, *, i, k_ref, vt_rows, vt_ref, qm, key_bias, slope, mask_fn):
    m, l, acc = carry
    start = pl.multiple_of(j * MOBA_BLOCK, MOBA_BLOCK)
    kj = k_ref[pl.ds(start, MOBA_BLOCK), :]
    u = lax.dot_general(kj, qm, _NT, preferred_element_type=F32) + key_bias
    u = mask_fn(u)
    cj = slope * ((j - i) * MOBA_BLOCK).astype(F32)
    m_new = jnp.maximum(m, jnp.max(u, axis=0, keepdims=True) + cj)
    p = jnp.exp(u - (m_new - cj))
    alpha = jnp.exp(m - m_new)
    l = alpha * l + jnp.sum(p, axis=0, keepdims=True)
    vtj = vt_ref[vt_rows, pl.ds(start, MOBA_BLOCK)]
    acc = alpha * acc + jnp.dot(vtj, p.astype(BF16), preferred_element_type=F32)
    return m_new, l, acc


def _flash_init(rows, tq):
    return (jnp.full((1, tq), NEG_INF, F32), jnp.zeros((1, tq), F32), jnp.zeros((rows, tq), F32))


def _moba_prompt_kernel(slopes_ref, q_ref, k_ref, vt_ref, kmean_ref, o_ref, qm_scr, sel_scr, *, nb):
    g = pl.program_id(1)
    i = pl.program_id(2)
    tq = q_ref.shape[0]
    heads = GROUP_LANES // HEAD_LANES
    q = q_ref[...]
    lane_head = lax.broadcasted_iota(jnp.int32, (1, GROUP_LANES), 1) // HEAD_LANES
    km = kmean_ref[...].astype(BF16)
    blk = lax.broadcasted_iota(jnp.int32, (nb, tq), 0)
    past = blk < i
    for h in range(heads):
        qm = jnp.where(lane_head == h, q, jnp.zeros_like(q))
        qm_scr[h] = qm
        gate = lax.dot_general(km, qm, _NT, preferred_element_type=F32)
        gate = jnp.where(past, gate, NEG_INF)
        cnt = jnp.zeros((nb, tq), jnp.int32)
        for jp in range(nb):
            gj = gate[jp:jp + 1, :]
            beats = (gj > gate) | ((gj == gate) & (blk > jp))
            cnt = cnt + beats.astype(jnp.int32)
        sel = (cnt < MOBA_TOPK) & past
        sel_scr[h] = jnp.where(sel, 0.0, NEG_INF).astype(F32)

    key_row = lax.broadcasted_iota(jnp.int32, (MOBA_BLOCK, tq), 0)
    qry_col = lax.broadcasted_iota(jnp.int32, (MOBA_BLOCK, tq), 1)
    causal = key_row <= qry_col
    key_off = key_row.astype(F32)
    outs = []
    for h in range(heads):
        slope = slopes_ref[g * heads + h]
        pair = h // 2
        step = functools.partial(
            _flash_step, i=i, k_ref=k_ref, vt_ref=vt_ref, vt_rows=pl.ds(pair * LANES, LANES),
            qm=qm_scr[h], key_bias=slope * key_off, slope=slope)
        carry = step(i, _flash_init(LANES, tq), mask_fn=lambda u: jnp.where(causal, u, NEG_INF))

        def body(j, c, step=step, h=h):
            return step(j, c, mask_fn=lambda u: u + sel_scr[h, pl.ds(j, 1), :])

        _, l, acc = lax.fori_loop(0, i, body, carry)
        half = (h % 2) * HEAD_LANES
        outs.append(acc[half:half + HEAD_LANES, :] / l)
    o_ref[...] = jnp.concatenate(outs, axis=0).T.astype(o_ref.dtype)


def _moba_prompt(q_bf, k_bf, vt_bf, kmean, slopes, *, bsz, seq):
    nq = seq // MOBA_BLOCK
    ng = WIDTH // GROUP_LANES
    heads = GROUP_LANES // HEAD_LANES
    return pl.pallas_call(
        functools.partial(_moba_prompt_kernel, nb=nq),
        grid=(bsz, ng, nq),
        in_specs=[_smem_spec(),
                  pl.BlockSpec((MOBA_BLOCK, GROUP_LANES), lambda b, g, i: (b * nq + i, g)),
                  pl.BlockSpec((seq, GROUP_LANES), lambda b, g, i: (b, g)),
                  pl.BlockSpec((GROUP_LANES, seq), lambda b, g, i: (g, b)),
                  pl.BlockSpec((None, nq, GROUP_LANES), lambda b, g, i: (b, 0, g))],
        out_specs=pl.BlockSpec((MOBA_BLOCK, GROUP_LANES), lambda b, g, i: (b * nq + i, g)),
        out_shape=jax.ShapeDtypeStruct((bsz * seq, WIDTH), BF16),
        scratch_shapes=[pltpu.VMEM((heads, MOBA_BLOCK, GROUP_LANES), BF16),
                        pltpu.VMEM((heads, nq, MOBA_BLOCK), F32)],
        compiler_params=pltpu.CompilerParams(dimension_semantics=("arbitrary",) * 3, vmem_limit_bytes=VMEM_LIMIT),
        name="moba_prompt",
    )(slopes, q_bf, k_bf, vt_bf, kmean)


def _diff_finish(o1, o2, lam, subw, lam_init):
    o = o1 - lam * o2
    o = o * lax.rsqrt(jnp.mean(o * o, axis=-1, keepdims=True) + LN_EPS)
    return o * subw * (1.0 - lam_init)


def _diff_prompt_kernel(slopes_ref, lam_ref, q_ref, k_ref, vt_ref, subw_ref, o_ref, *, lam_init):
    g = pl.program_id(1)
    i = pl.program_id(2)
    tq = q_ref.shape[0]
    heads = GROUP_LANES // (2 * HEAD_LANES)
    q = q_ref[...]
    lane_map = lax.broadcasted_iota(jnp.int32, (1, GROUP_LANES), 1) // HEAD_LANES
    key_row = lax.broadcasted_iota(jnp.int32, (MOBA_BLOCK, tq), 0)
    qry_col = lax.broadcasted_iota(jnp.int32, (MOBA_BLOCK, tq), 1)
    causal = key_row <= qry_col
    key_off = key_row.astype(F32)
    lam = lam_ref[0]
    for h in range(heads):
        slope = slopes_ref[g * heads + h]
        maps = []
        for mm in range(2):
            qm = jnp.where(lane_map == 2 * h + mm, q, jnp.zeros_like(q))
            step = functools.partial(
                _flash_step, i=i, k_ref=k_ref, vt_ref=vt_ref, vt_rows=pl.ds(h * B_V_DIM, B_V_DIM),
                qm=qm, key_bias=slope * key_off, slope=slope)
            carry = step(i, _flash_init(B_V_DIM, tq), mask_fn=lambda u: jnp.where(causal, u, NEG_INF))
            _, l, acc = lax.fori_loop(0, i, functools.partial(step, mask_fn=lambda u: u), carry)
            maps.append((acc / l).T)
        o = _diff_finish(maps[0], maps[1], lam, subw_ref[...], lam_init)
        o_ref[:, h * B_V_DIM:(h + 1) * B_V_DIM] = o.astype(o_ref.dtype)


def _diff_prompt(q_bf, k_bf, vt_bf, slopes, lam, subw, *, bsz, seq, lam_init):
    nq = seq // MOBA_BLOCK
    ng = WIDTH // GROUP_LANES
    return pl.pallas_call(
        functools.partial(_diff_prompt_kernel, lam_init=lam_init),
        grid=(bsz, ng, nq),
        in_specs=[_smem_spec(), _smem_spec(),
                  pl.BlockSpec((MOBA_BLOCK, GROUP_LANES), lambda b, g, i: (b * nq + i, g)),
                  pl.BlockSpec((seq, GROUP_LANES), lambda b, g, i: (b, g)),
                  pl.BlockSpec((GROUP_LANES, seq), lambda b, g, i: (g, b)),
                  pl.BlockSpec((1, B_V_DIM), lambda b, g, i: (0, 0))],
        out_specs=pl.BlockSpec((MOBA_BLOCK, GROUP_LANES), lambda b, g, i: (b * nq + i, g)),
        out_shape=jax.ShapeDtypeStruct((bsz * seq, WIDTH), BF16),
        compiler_params=pltpu.CompilerParams(dimension_semantics=("arbitrary",) * 3, vmem_limit_bytes=VMEM_LIMIT),
        name="diff_prompt",
    )(slopes, lam, q_bf, k_bf, vt_bf, subw)


def _head_rows(x, n_new):
    rows = (WIDTH // HEAD_LANES) * n_new
    tiled = jnp.concatenate([x] * (WIDTH // HEAD_LANES), axis=0)
    r = lax.broadcasted_iota(jnp.int32, (rows, WIDTH), 0) // n_new
    c = lax.broadcasted_iota(jnp.int32, (rows, WIDTH), 1) // HEAD_LANES
    return jnp.where(r == c, tiled, 0.0).astype(BF16)


def _head_diag(full, n_new, chunk_lanes, first, stride):
    lane_chunk = lax.broadcasted_iota(jnp.int32, (n_new, WIDTH), 1) // chunk_lanes
    out = jnp.zeros((n_new, WIDTH), F32)
    for c in range(WIDTH // chunk_lanes):
        grp = first + stride * c
        out = out + jnp.where(lane_chunk == c, full[grp * n_new:(grp + 1) * n_new, :], 0.0)
    return out


def _col(x):
    return x[:, :1]


def _sample_attn_kernel(pt_ref, lam_ref, qa_ref, qb_ref, kan_ref, van_ref, kbn_ref, vbn_ref, subw_ref,
                        sla_ref, slb_ref, ak0, ak1, av0, av1, bk0, bk1, bv0, bv1, oa_ref, ob_ref,
                        qa_scr, qb_scr, kmean_scr, mst_scr, lst_scr, acc_scr, dm_scr, dl_scr, dacc_scr,
                        *, nblk, n_new, lam_init):
    del pt_ref
    j = pl.program_id(1)
    rows = qa_scr.shape[0]
    past_len = nblk * MOBA_BLOCK

    @pl.when(j == 0)
    def _():
        qa_scr[...] = _head_rows(qa_ref[...], n_new)
        qb_scr[...] = _head_rows(qb_ref[...], n_new)
        dm_scr[...] = jnp.full(dm_scr.shape, NEG_INF, F32)
        dl_scr[...] = jnp.zeros(dl_scr.shape, F32)
        dacc_scr[...] = jnp.zeros(dacc_scr.shape, F32)

    key_off = lax.broadcasted_iota(jnp.int32, (1, MOBA_BLOCK), 1).astype(F32)
    blk_off = (j * MOBA_BLOCK - past_len).astype(F32)
    sla = sla_ref[...]
    slb = slb_ref[...]

    ka = jnp.concatenate([ak0[...], ak1[...]], axis=0)
    kmean_scr[pl.ds(j, 1), :] = jnp.mean(ka, axis=0, keepdims=True)
    va = jnp.concatenate([av0[...], av1[...]], axis=0).astype(BF16)
    s = lax.dot_general(qa_scr[...], ka.astype(BF16), _NT, preferred_element_type=F32) + _col(sla) * key_off
    m_loc = jnp.max(s, axis=1, keepdims=True)
    p = jnp.exp(s - m_loc)
    mst_scr[j] = jnp.broadcast_to(m_loc, (rows, LANES)) + sla * blk_off
    lst_scr[j] = jnp.broadcast_to(jnp.sum(p, axis=1, keepdims=True), (rows, LANES))
    acc_scr[j] = jnp.dot(p.astype(BF16), va, preferred_element_type=F32)

    kb = jnp.concatenate([bk0[...], bk1[...]], axis=0).astype(BF16)
    vb = jnp.concatenate([bv0[...], bv1[...]], axis=0).astype(BF16)
    s = lax.dot_general(qb_scr[...], kb, _NT, preferred_element_type=F32) + _col(slb) * key_off
    cj = _col(slb) * blk_off
    m_old = _col(dm_scr[...])
    m_new = jnp.maximum(m_old, jnp.max(s, axis=1, keepdims=True) + cj)
    p = jnp.exp(s - (m_new - cj))
    alpha = jnp.exp(m_old - m_new)
    dl_scr[...] = jnp.broadcast_to(alpha * _col(dl_scr[...]) + jnp.sum(p, axis=1, keepdims=True), (rows, LANES))
    dacc_scr[...] = alpha * dacc_scr[...] + jnp.dot(p.astype(BF16), vb, preferred_element_type=F32)
    dm_scr[...] = jnp.broadcast_to(m_new, (rows, LANES))

    @pl.when(j == nblk - 1)
    def _():
        pad = jnp.zeros((LANES - n_new, WIDTH), F32)
        new_key = lax.broadcasted_iota(jnp.int32, (rows, LANES), 1)
        qry = lax.broadcasted_iota(jnp.int32, (rows, LANES), 0) % n_new
        causal = new_key <= qry
        new_off = new_key.astype(F32)

        def new_block(q_rows, k_new, v_new, slope):
            kn = jnp.concatenate([k_new, pad], axis=0).astype(BF16)
            vn = jnp.concatenate([v_new, pad], axis=0).astype(BF16)
            sn = lax.dot_general(q_rows, kn, _NT, preferred_element_type=F32) + slope * new_off
            return jnp.where(causal, sn, NEG_INF), vn

        qa_rows = qa_scr[...]
        sn, vn = new_block(qa_rows, kan_ref[...], van_ref[...], sla)
        m_own = jnp.max(sn, axis=1, keepdims=True)
        pn = jnp.exp(sn - m_own)
        l_own = jnp.sum(pn, axis=1, keepdims=True)
        acc_own = jnp.dot(pn.astype(BF16), vn, preferred_element_type=F32)
        gate = lax.dot_general(qa_rows, kmean_scr[...].astype(BF16), _NT, preferred_element_type=F32)
        blk = lax.broadcasted_iota(jnp.int32, (rows, nblk), 1)
        cnt = jnp.zeros((rows, nblk), jnp.int32)
        for jp in range(nblk):
            gj = gate[:, jp:jp + 1]
            beats = (gj > gate) | ((gj == gate) & (blk > jp))
            cnt = cnt + beats.astype(jnp.int32)
        sel = jnp.where(cnt < MOBA_TOPK, 1.0, 0.0).astype(F32)
        m_all = jnp.broadcast_to(m_own, (rows, LANES))
        for jb in range(nblk):
            pick = jnp.broadcast_to(sel[:, jb:jb + 1], (rows, LANES)) > 0.0
            m_all = jnp.maximum(m_all, jnp.where(pick, mst_scr[jb], NEG_INF))
        w_own = jnp.exp(m_own - _col(m_all))
        l_all = w_own * l_own
        acc = w_own * acc_own
        for jb in range(nblk):
            pick = jnp.broadcast_to(sel[:, jb:jb + 1], (rows, LANES)) > 0.0
            w = jnp.where(pick, jnp.exp(mst_scr[jb] - m_all), 0.0)
            l_all = l_all + _col(w * lst_scr[jb])
            acc = acc + _col(w) * acc_scr[jb]
        oa_ref[...] = _head_diag(acc / l_all, n_new, HEAD_LANES, 0, 1)

        sn, vn = new_block(qb_scr[...], kbn_ref[...], vbn_ref[...], slb)
        m_old = _col(dm_scr[...])
        m_new = jnp.maximum(m_old, jnp.max(sn, axis=1, keepdims=True))
        pn = jnp.exp(sn - m_new)
        alpha = jnp.exp(m_old - m_new)
        l_d = alpha * _col(dl_scr[...]) + jnp.sum(pn, axis=1, keepdims=True)
        acc_d = alpha * dacc_scr[...] + jnp.dot(pn.astype(BF16), vn, preferred_element_type=F32)
        full = acc_d / l_d
        o1 = _head_diag(full, n_new, B_V_DIM, 0, 2)
        o2 = _head_diag(full, n_new, B_V_DIM, 1, 2)
        lam = lam_ref[0]
        for h in range(B_HEADS):
            cols = slice(h * B_V_DIM, (h + 1) * B_V_DIM)
            ob_ref[:, cols] = _diff_finish(o1[:, cols], o2[:, cols], lam, subw_ref[...], lam_init)


def _sample_attention(qa, ka, va, qb, kb, vb, pool_ak, pool_av, pool_bk, pool_bv, page_table,
                      slopes_a, slopes_b, lam, subw, *, lam_init):
    n_seq, n_pages = page_table.shape
    n_new = qa.shape[0] // n_seq
    page = pool_ak.shape[1]
    pages_per_blk = MOBA_BLOCK // page
    assert pages_per_blk == 2 and n_pages % pages_per_blk == 0 and n_new <= LANES
    nblk = n_pages // pages_per_blk
    rows = (WIDTH // HEAD_LANES) * n_new
    n_pool = pool_ak.shape[0]
    pools = [p.reshape(n_pool, page, WIDTH) for p in (pool_ak, pool_av, pool_bk, pool_bv)]
    per_seq = [x.reshape(n_seq, n_new, WIDTH) for x in (qa, qb, ka, va, kb, vb)]
    row_slope_a = jnp.broadcast_to(jnp.repeat(slopes_a, n_new)[:, None], (rows, LANES))
    row_slope_b = jnp.broadcast_to(jnp.repeat(slopes_b, 2 * n_new)[:, None], (rows, LANES))

    seq_spec = pl.BlockSpec((None, n_new, WIDTH), lambda s, j, pt: (s, 0, 0))
    const2 = lambda s, j, pt: (0, 0)

    def page_spec(which):
        return pl.BlockSpec((None, page, WIDTH), lambda s, j, pt: (pt[s * n_pages + pages_per_blk * j + which], 0, 0))

    cache_specs, cache_args = [], []
    for pool in pools:
        for which in range(pages_per_blk):
            cache_specs.append(page_spec(which))
            cache_args.append(pool)
    grid_spec = pltpu.PrefetchScalarGridSpec(
        num_scalar_prefetch=1,
        grid=(n_seq, nblk),
        in_specs=[_smem_spec()] + [seq_spec] * 6
                 + [pl.BlockSpec((1, B_V_DIM), const2), pl.BlockSpec((rows, LANES), const2),
                    pl.BlockSpec((rows, LANES), const2)] + cache_specs,
        out_specs=[seq_spec, seq_spec],
        scratch_shapes=[pltpu.VMEM((rows, WIDTH), BF16), pltpu.VMEM((rows, WIDTH), BF16),
                        pltpu.VMEM((nblk, WIDTH), F32),
                        pltpu.VMEM((nblk, rows, LANES), F32), pltpu.VMEM((nblk, rows, LANES), F32),
                        pltpu.VMEM((nblk, rows, WIDTH), F32),
                        pltpu.VMEM((rows, LANES), F32), pltpu.VMEM((rows, LANES), F32),
                        pltpu.VMEM((rows, WIDTH), F32)])
    o_a, o_b = pl.pallas_call(
        functools.partial(_sample_attn_kernel, nblk=nblk, n_new=n_new, lam_init=lam_init),
        grid_spec=grid_spec,
        out_shape=[jax.ShapeDtypeStruct((n_seq, n_new, WIDTH), F32)] * 2,
        compiler_params=pltpu.CompilerParams(dimension_semantics=("arbitrary", "arbitrary"),
                                             vmem_limit_bytes=VMEM_LIMIT),
        name="sample_attention",
    )(page_table.reshape(-1), lam, *per_seq, subw, row_slope_a, row_slope_b, *cache_args)
    return o_a.reshape(n_seq * n_new, WIDTH), o_b.reshape(n_seq * n_new, WIDTH)


def _route(logits):
    lane_i = lax.broadcasted_iota(jnp.int32, logits.shape, 1)
    lane = lane_i.astype(F32)
    lane_grp = (lane_i // EXPERTS_PER_GROUP).astype(F32)

    def top(vals):
        v = jnp.max(vals, axis=1, keepdims=True)
        idx = jnp.min(jnp.where(vals == v, lane, float(LANES)), axis=1, keepdims=True)
        return v, idx

    is_grp = (lane_i >= N_EXPERTS) & (lane_i < N_EXPERTS + N_GROUPS)
    lg = jnp.where(is_grp, logits, NEG_INF)
    g_max, g_idx = top(lg)
    p_grp = 1.0 / jnp.sum(jnp.exp(lg - g_max), axis=1, keepdims=True)
    g_sel = g_idx - float(N_EXPERTS)
    in_grp = (lane_i < N_EXPERTS) & (lane_grp == g_sel)
    le = jnp.where(in_grp, logits, NEG_INF)
    v1, i1 = top(le)
    v2, i2 = top(jnp.where(lane == i1, NEG_INF, le))
    e2 = jnp.exp(v2 - v1)
    w1 = 1.0 / (1.0 + e2)
    w2 = e2 / (1.0 + e2)
    return jnp.where(lane == i1, w1 * p_grp, 0.0) + jnp.where(lane == i2, w2 * p_grp, 0.0)


def _merge_kernel(x_ref, oa_ref, ob_ref, g_ref, wpa_ref, wpb_ref, wo_ref, lng_ref, lnb_ref, wr_ref, br_ref,
                  h_ref, comb_ref, *, alpha):
    d = x_ref.shape[1]
    ya = jnp.dot(oa_ref[...].astype(BF16), wpa_ref[...], preferred_element_type=F32)
    yb = jnp.dot(ob_ref[...].astype(BF16), wpb_ref[...], preferred_element_type=F32)
    y = g_ref[:, :d] * ya + g_ref[:, d:] * yb
    mix = jnp.dot(y.astype(BF16), wo_ref[...], preferred_element_type=F32)
    h = _layer_norm(alpha * x_ref[...] + mix, lng_ref[...], lnb_ref[...])
    h_ref[...] = h
    logits = jnp.dot(h.astype(BF16), wr_ref[...], preferred_element_type=F32) + br_ref[...]
    comb_ref[...] = _route(logits)


def _merge(x, o_a, o_b, gates, wpa, wpb, wo, ln_g, ln_b, w_router, b_router, *, alpha, name):
    t, d = x.shape
    tm = 256
    row = lambda i: (i, 0)
    const = lambda i: (0, 0)
    return pl.pallas_call(
        functools.partial(_merge_kernel, alpha=alpha),
        grid=(t // tm,),
        in_specs=[pl.BlockSpec((tm, d), row), pl.BlockSpec((tm, WIDTH), row), pl.BlockSpec((tm, WIDTH), row),
                  pl.BlockSpec((tm, 2 * d), row),
                  pl.BlockSpec((WIDTH, d), const), pl.BlockSpec((WIDTH, d), const), pl.BlockSpec((d, d), const),
                  pl.BlockSpec((1, d), const), pl.BlockSpec((1, d), const),
                  pl.BlockSpec((d, LANES), const), pl.BlockSpec((1, LANES), const)],
        out_specs=[pl.BlockSpec((tm, d), row), pl.BlockSpec((tm, LANES), row)],
        out_shape=[jax.ShapeDtypeStruct((t, d), F32), jax.ShapeDtypeStruct((t, LANES), F32)],
        compiler_params=pltpu.CompilerParams(dimension_semantics=("arbitrary",), vmem_limit_bytes=VMEM_LIMIT),
        name=name,
    )(x, o_a, o_b, gates, wpa, wpb, wo, ln_g, ln_b, w_router, b_router)


def _moe_kernel(h_ref, comb_ref, wgu_ref, wd_ref, lng_ref, lnb_ref, o_ref, hb_scr, acc_scr, *, alpha):
    e = pl.program_id(1)
    f = wd_ref.shape[0]

    @pl.when(e == 0)
    def _():
        hb_scr[...] = h_ref[...].astype(BF16)
        acc_scr[...] = jnp.zeros(acc_scr.shape, F32)

    gu = jnp.dot(hb_scr[...], wgu_ref[...], preferred_element_type=F32)
    gate, up = gu[:, :f], gu[:, f:]
    comb = comb_ref[...]
    lane = lax.broadcasted_iota(jnp.int32, comb.shape, 1)
    c = jnp.sum(jnp.where(lane == e, comb, 0.0), axis=1, keepdims=True)
    act = gate / (1.0 + jnp.exp(-gate)) * up * c
    acc_scr[...] += jnp.dot(act.astype(BF16), wd_ref[...], preferred_element_type=F32)

    @pl.when(e == pl.num_programs(1) - 1)
    def _():
        o_ref[...] = _layer_norm(alpha * h_ref[...] + acc_scr[...], lng_ref[...], lnb_ref[...])


def _moe(h, comb, wgu, wd, ln_g, ln_b, *, alpha, name):
    t, d = h.shape
    tm = next(c for c in (1024, 512, 256) if t % c == 0)
    n_e, _, f2 = wgu.shape
    row = lambda i, e: (i, 0)
    const = lambda i, e: (0, 0)
    return pl.pallas_call(
        functools.partial(_moe_kernel, alpha=alpha),
        grid=(t // tm, n_e),
        in_specs=[pl.BlockSpec((tm, d), row), pl.BlockSpec((tm, LANES), row),
                  pl.BlockSpec((None, d, f2), lambda i, e: (e, 0, 0)),
                  pl.BlockSpec((None, f2 // 2, d), lambda i, e: (e, 0, 0)),
                  pl.BlockSpec((1, d), const), pl.BlockSpec((1, d), const)],
        out_specs=pl.BlockSpec((tm, d), row),
        out_shape=jax.ShapeDtypeStruct((t, d), F32),
        scratch_shapes=[pltpu.VMEM((tm, d), BF16), pltpu.VMEM((tm, d), F32)],
        compiler_params=pltpu.CompilerParams(dimension_semantics=("arbitrary", "arbitrary"),
                                             vmem_limit_bytes=VMEM_LIMIT),
        name=name,
    )(h, comb, wgu, wd, ln_g, ln_b)


def kernel(x_prompt, x_sample, cache_moba_k, cache_moba_v, cache_diff_k, cache_diff_v, page_table, w_in, b_gate, lambda_q1, lambda_k1, lambda_q2, lambda_k2, subln_w, w_proj_a, w_proj_b, w_out, ln1_g, ln1_b, ln2_g, ln2_b, w_route_group, b_route_group, w_route_expert, b_route_expert, w_expert_gate, w_expert_up, w_expert_down):
    depth = w_in.shape[0]
    bsz, seq, d = x_prompt.shape
    n_seq, n_new, _ = x_sample.shape
    alpha = (2 * depth) ** 0.25
    slopes_a = _alibi_slopes(A_HEADS)
    slopes_b = _alibi_slopes(B_HEADS)
    h_p = x_prompt.reshape(bsz * seq, d)
    h_s = x_sample.reshape(n_seq * n_new, d)
    rows_p, rows_s = [], []
    for l in range(depth):
        lam_init = _lambda_init(l)
        lam = (jnp.exp(jnp.sum(lambda_q1[l].astype(F32) * lambda_k1[l].astype(F32)))
               - jnp.exp(jnp.sum(lambda_q2[l].astype(F32) * lambda_k2[l].astype(F32))) + lam_init).reshape(1)
        w_bf = w_in[l].astype(BF16)
        wvta = w_bf[:, 2 * WIDTH:3 * WIDTH].T
        wvtb = w_bf[:, 5 * WIDTH:6 * WIDTH].T
        subw = subln_w[l].reshape(1, B_V_DIM)
        wpa, wpb, wo = w_proj_a[l].astype(BF16), w_proj_b[l].astype(BF16), w_out[l].astype(BF16)
        n_route = N_EXPERTS + N_GROUPS
        w_router = jnp.pad(jnp.concatenate([w_route_expert[l], w_route_group[l]], axis=1),
                           ((0, 0), (0, LANES - n_route))).astype(BF16)
        b_router = jnp.pad(jnp.concatenate([b_route_expert[l], b_route_group[l]]), (0, LANES - n_route)).reshape(1, LANES)
        wgu = jnp.concatenate([w_expert_gate[l], w_expert_up[l]], axis=2).astype(BF16)
        wd = w_expert_down[l].astype(BF16)
        ln1 = (ln1_g[l].reshape(1, d), ln1_b[l].reshape(1, d))
        ln2 = (ln2_g[l].reshape(1, d), ln2_b[l].reshape(1, d))

        (qa, ka, va, qb, kb, vb, gates, ka_bf, kb_bf, vat, vbt, kmean) = _in_projection(
            h_p, w_bf, b_gate[l], wvta, wvtb, prompt=True)
        o_a = _moba_prompt(qa, ka_bf, vat, kmean.reshape(bsz, seq // MOBA_BLOCK, WIDTH), slopes_a, bsz=bsz, seq=seq)
        o_b = _diff_prompt(qb, kb_bf, vbt, slopes_b, lam, subw, bsz=bsz, seq=seq, lam_init=lam_init)
        h1, comb = _merge(h_p, o_a, o_b, gates, wpa, wpb, wo, *ln1, w_router, b_router, alpha=alpha,
                          name="merge_prompt")
        h_p = _moe(h1, comb, wgu, wd, *ln2, alpha=alpha, name="moe_prompt")
        rows_p.append((ka.reshape(bsz, seq, A_HEADS, A_HEAD_DIM), va.reshape(bsz, seq, A_HEADS, A_HEAD_DIM),
                       kb.reshape(bsz, seq, B_HEADS, 2 * B_HEAD_DIM), vb.reshape(bsz, seq, B_HEADS, B_V_DIM)))

        qa, ka, va, qb, kb, vb, gates = _in_projection(h_s, w_bf, b_gate[l], None, None, prompt=False)
        o_a, o_b = _sample_attention(qa, ka, va, qb, kb, vb, cache_moba_k[l], cache_moba_v[l], cache_diff_k[l],
                                     cache_diff_v[l], page_table, slopes_a, slopes_b, lam, subw, lam_init=lam_init)
        h1, comb = _merge(h_s, o_a, o_b, gates, wpa, wpb, wo, *ln1, w_router, b_router, alpha=alpha,
                          name="merge_sample")
        h_s = _moe(h1, comb, wgu, wd, *ln2, alpha=alpha, name="moe_sample")
        rows_s.append((ka.reshape(n_seq, n_new, A_HEADS, A_HEAD_DIM), va.reshape(n_seq, n_new, A_HEADS, A_HEAD_DIM),
                       kb.reshape(n_seq, n_new, B_HEADS, 2 * B_HEAD_DIM), vb.reshape(n_seq, n_new, B_HEADS, B_V_DIM)))

    stack = lambda rows, k: jnp.stack([r[k] for r in rows])
    return (h_p.reshape(bsz, seq, d), h_s.reshape(n_seq, n_new, d),
            stack(rows_p, 0), stack(rows_p, 1), stack(rows_p, 2), stack(rows_p, 3),
            stack(rows_s, 0), stack(rows_s, 1), stack(rows_s, 2), stack(rows_s, 3))
```

```python
import functools
import math

import numpy as np
import jax
import jax.numpy as jnp
from jax import lax
from jax.experimental import pallas as pl
from jax.experimental.pallas import tpu as pltpu

F32 = jnp.float32
BF16 = jnp.bfloat16

A_HEADS = 8
A_HEAD_DIM = 64
B_HEADS = 4
B_HEAD_DIM = 64
B_V_DIM = 2 * B_HEAD_DIM
WIDTH = A_HEADS * A_HEAD_DIM
MOBA_BLOCK = 256
MOBA_TOPK = 3
N_GROUPS = 4
EXPERTS_PER_GROUP = 4
N_EXPERTS = N_GROUPS * EXPERTS_PER_GROUP
LN_EPS = 1e-5
LANES = 128
HEAD_LANES = 64
GROUP_LANES = 256
CHAINS = GROUP_LANES // HEAD_LANES
NEG_INF = float("-inf")
POS_INF = float("inf")
LOG2E = math.log2(math.e)
VMEM_LIMIT = 56 * 1024 * 1024
SAMPLE_PAGES_PER_STEP = 8

_NT = (((1,), (1,)), ((), ()))


def _lambda_init(layer):
    return 0.8 - 0.6 * math.exp(-0.3 * layer)


def _alibi_slopes(n_heads):
    return np.power(2.0, -8.0 * np.arange(1, n_heads + 1) / n_heads)


def _smem_spec():
    return pl.BlockSpec(memory_space=pltpu.SMEM)


def _layer_norm(x, g, b):
    mu = jnp.mean(x, axis=-1, keepdims=True)
    xc = x - mu
    var = jnp.mean(xc * xc, axis=-1, keepdims=True)
    return xc * lax.rsqrt(var + LN_EPS) * g + b


def _dot(a, b):
    return jnp.dot(a, b, preferred_element_type=F32)


def _dot_nt(a, b):
    return lax.dot_general(a, b, _NT, preferred_element_type=F32)


def _inproj_kernel(x_ref, w_ref, bg_ref, *refs, prompt):
    if prompt:
        (wkat_ref, wvat_ref, wvbt_ref, qa_ref, qb_ref, kb_ref, vb_ref, g_ref,
         kab_ref, kbb_ref, kat_ref, vat_ref, vatb_ref, vbtb_ref, kmean_ref) = refs
    else:
        qa_ref, ka_ref, va_ref, qb_ref, kb_ref, vb_ref, g_ref = refs
    xb = x_ref[...].astype(BF16)

    def proj(slot):
        return _dot(xb, w_ref[:, slot * WIDTH:(slot + 1) * WIDTH])

    q_scale = A_HEAD_DIM ** -0.5 * LOG2E
    qa_ref[...] = (proj(0) * q_scale).astype(qa_ref.dtype)
    qb_ref[...] = (proj(3) * q_scale).astype(qb_ref.dtype)
    ka = proj(1)
    kb = proj(4)
    kb_ref[...] = kb
    vb_ref[...] = proj(5)
    z = _dot(xb, w_ref[:, 6 * WIDTH:]) + bg_ref[...]
    g_ref[...] = 1.0 / (1.0 + jnp.exp(-z))
    if prompt:
        kab_ref[...] = ka.astype(BF16)
        kbb_ref[...] = kb.astype(BF16)
        kmean_ref[...] = jnp.mean(ka, axis=0, keepdims=True)
        kat_ref[...] = _dot_nt(wkat_ref[...], xb)
        vat = _dot_nt(wvat_ref[...], xb)
        vat_ref[...] = vat
        vatb_ref[...] = vat.astype(BF16)
        vbtb_ref[...] = _dot_nt(wvbt_ref[...], xb).astype(BF16)
    else:
        ka_ref[...] = ka
        va_ref[...] = proj(2)


def _in_projection(x, w_bf, b_gate, w_t, *, prompt, bsz=1):
    t, d = x.shape
    tm = MOBA_BLOCK
    assert t % (bsz * tm) == 0
    seq = t // bsz
    nt = seq // tm
    n_in = w_bf.shape[1]
    row = lambda i: (i, 0)
    const = lambda i: (0, 0)
    rows_f32 = jax.ShapeDtypeStruct((t, WIDTH), F32)
    rows_bf = jax.ShapeDtypeStruct((t, WIDTH), BF16)
    row_spec = pl.BlockSpec((tm, WIDTH), row)
    in_specs = [pl.BlockSpec((tm, d), row), pl.BlockSpec((d, n_in), const), pl.BlockSpec((1, 2 * d), const)]
    args = [x, w_bf, b_gate.reshape(1, 2 * d)]
    gate_shape = jax.ShapeDtypeStruct((t, 2 * d), F32)
    gate_spec = pl.BlockSpec((tm, 2 * d), row)
    if prompt:
        in_specs += [pl.BlockSpec((WIDTH, d), const)] * 3
        args += list(w_t)
        feat_f32 = jax.ShapeDtypeStruct((bsz, WIDTH, seq), F32)
        feat_bf = jax.ShapeDtypeStruct((bsz, WIDTH, seq), BF16)
        feat_spec = pl.BlockSpec((None, WIDTH, tm), lambda i: (i // nt, 0, i % nt))
        out_shape = [rows_bf, rows_bf, rows_f32, rows_f32, gate_shape, rows_bf, rows_bf,
                     feat_f32, feat_f32, feat_bf, feat_bf, jax.ShapeDtypeStruct((t // tm, 1, WIDTH), F32)]
        out_specs = [row_spec] * 4 + [gate_spec] + [row_spec] * 2 + [feat_spec] * 4
        out_specs += [pl.BlockSpec((None, 1, WIDTH), lambda i: (i, 0, 0))]
    else:
        out_shape = [rows_f32] * 6 + [gate_shape]
        out_specs = [row_spec] * 6 + [gate_spec]
    return pl.pallas_call(
        functools.partial(_inproj_kernel, prompt=prompt),
        grid=(t // tm,),
        in_specs=in_specs, out_specs=out_specs, out_shape=out_shape,
        compiler_params=pltpu.CompilerParams(dimension_semantics=("arbitrary",), vmem_limit_bytes=VMEM_LIMIT),
        name="in_projection_prompt" if prompt else "in_projection_sample",
    )(*args)


def _attend_tile(i, k_ref, vt_ref, qm_scr, bias_scr, m_scr, l_scr, acc_scr, slopes, vt_rows, sel_row):
    tq = qm_scr.shape[1]
    key_row = lax.broadcasted_iota(jnp.int32, (MOBA_BLOCK, tq), 0)
    causal = key_row <= lax.broadcasted_iota(jnp.int32, (MOBA_BLOCK, tq), 1)
    key_off = key_row.astype(F32)
    for c in range(CHAINS):
        bias_scr[c] = slopes[c] * key_off
        m_scr[c] = jnp.full((1, tq), NEG_INF, F32)
        l_scr[c] = jnp.zeros((1, tq), F32)
        acc_scr[c] = jnp.zeros(acc_scr.shape[1:], F32)

    def block(j, own):
        start = pl.multiple_of(j * MOBA_BLOCK, MOBA_BLOCK)
        kj = k_ref[pl.ds(start, MOBA_BLOCK), :]
        for c in range(CHAINS):
            u = _dot_nt(kj, qm_scr[c]) + bias_scr[c]
            m_old = m_scr[c]
            if own:
                u = jnp.where(causal, u, NEG_INF)
                m_new = jnp.maximum(m_old, jnp.max(u, axis=0, keepdims=True))
                shift = m_new
            else:
                cj = slopes[c] * ((j - i) * MOBA_BLOCK).astype(F32)
                m_blk = jnp.max(u, axis=0, keepdims=True) + cj
                pick = sel_row(c, j)
                if pick is not None:
                    m_blk = jnp.where(pick > 0.0, m_blk, NEG_INF)
                m_new = jnp.maximum(m_old, m_blk)
                shift = m_new - cj
                if pick is not None:
                    shift = jnp.where(pick > 0.0, shift, POS_INF)
            p = jnp.exp2(u - shift)
            alpha = jnp.exp2(m_old - m_new)
            m_scr[c] = m_new
            l_scr[c] = alpha * l_scr[c] + jnp.sum(p, axis=0, keepdims=True)
            vtj = vt_ref[vt_rows[c], pl.ds(start, MOBA_BLOCK)]
            acc_scr[c] = alpha * acc_scr[c] + _dot(vtj, p.astype(BF16))

    block(i, True)

    def body(j, carry):
        block(j, False)
        return carry

    lax.fori_loop(0, i, body, 0)


def _moba_prompt_kernel(slopes_ref, q_ref, k_ref, vt_ref, kmean_ref, o_ref,
                        qm_scr, sel_scr, bias_scr, m_scr, l_scr, acc_scr, *, nb):
    g = pl.program_id(1)
    i = pl.program_id(2)
    tq = q_ref.shape[0]
    q = q_ref[...]
    lane_head = lax.broadcasted_iota(jnp.int32, (1, GROUP_LANES), 1) // HEAD_LANES
    km = kmean_ref[...].astype(BF16)
    blk = lax.broadcasted_iota(jnp.int32, (nb, tq), 0)
    past = blk < i
    for h in range(CHAINS):
        qm = jnp.where(lane_head == h, q, jnp.zeros_like(q))
        qm_scr[h] = qm
        gate = jnp.where(past, _dot_nt(km, qm), NEG_INF)
        cnt = jnp.zeros((nb, tq), jnp.int32)
        for jp in range(nb):
            gj = gate[jp:jp + 1, :]
            beats = (gj > gate) | ((gj == gate) & (blk > jp))
            cnt = cnt + beats.astype(jnp.int32)
        sel_scr[h] = jnp.where((cnt < MOBA_TOPK) & past, 1.0, 0.0).astype(F32)

    slopes = [slopes_ref[g * CHAINS + h] for h in range(CHAINS)]
    vt_rows = [pl.ds((h // 2) * LANES, LANES) for h in range(CHAINS)]
    _attend_tile(i, k_ref, vt_ref, qm_scr, bias_scr, m_scr, l_scr, acc_scr, slopes, vt_rows,
                 lambda c, j: sel_scr[c, pl.ds(j, 1), :])
    outs = []
    for h in range(CHAINS):
        half = (h % 2) * HEAD_LANES
        outs.append(acc_scr[h, half:half + HEAD_LANES, :] * (1.0 / l_scr[h]))
    o_ref[...] = jnp.concatenate(outs, axis=0).T.astype(o_ref.dtype)


def _attn_scratch(tq, acc_rows):
    return [pltpu.VMEM((CHAINS, MOBA_BLOCK, tq), F32), pltpu.VMEM((CHAINS, 1, tq), F32),
            pltpu.VMEM((CHAINS, 1, tq), F32), pltpu.VMEM((CHAINS, acc_rows, tq), F32)]


def _attn_specs(bsz, seq):
    nq = seq // MOBA_BLOCK
    q_spec = pl.BlockSpec((MOBA_BLOCK, GROUP_LANES), lambda b, g, i: (b * nq + i, g))
    k_spec = pl.BlockSpec((seq, GROUP_LANES), lambda b, g, i: (b, g))
    vt_spec = pl.BlockSpec((None, GROUP_LANES, seq), lambda b, g, i: (b, g, 0))
    return nq, q_spec, k_spec, vt_spec


def _moba_prompt(q_bf, k_bf, vt_bf, kmean, slopes, *, bsz, seq):
    nq, q_spec, k_spec, vt_spec = _attn_specs(bsz, seq)
    return pl.pallas_call(
        functools.partial(_moba_prompt_kernel, nb=nq),
        grid=(bsz, WIDTH // GROUP_LANES, nq),
        in_specs=[_smem_spec(), q_spec, k_spec, vt_spec,
                  pl.BlockSpec((None, nq, GROUP_LANES), lambda b, g, i: (b, 0, g))],
        out_specs=q_spec,
        out_shape=jax.ShapeDtypeStruct((bsz * seq, WIDTH), BF16),
        scratch_shapes=[pltpu.VMEM((CHAINS, MOBA_BLOCK, GROUP_LANES), BF16),
                        pltpu.VMEM((CHAINS, nq, MOBA_BLOCK), F32)] + _attn_scratch(MOBA_BLOCK, LANES),
        compiler_params=pltpu.CompilerParams(dimension_semantics=("arbitrary",) * 3, vmem_limit_bytes=VMEM_LIMIT),
        name="moba_prompt",
    )(slopes, q_bf, k_bf, vt_bf, kmean)


def _diff_finish(o1, o2, lam, subw, lam_init):
    o = o1 - lam * o2
    o = o * lax.rsqrt(jnp.mean(o * o, axis=-1, keepdims=True) + LN_EPS)
    return o * subw * (1.0 - lam_init)


def _diff_prompt_kernel(slopes_ref, lam_ref, q_ref, k_ref, vt_ref, subw_ref, o_ref,
                        qm_scr, bias_scr, m_scr, l_scr, acc_scr, *, lam_init):
    g = pl.program_id(1)
    i = pl.program_id(2)
    heads = CHAINS // 2
    q = q_ref[...]
    lane_map = lax.broadcasted_iota(jnp.int32, (1, GROUP_LANES), 1) // HEAD_LANES
    for c in range(CHAINS):
        qm_scr[c] = jnp.where(lane_map == c, q, jnp.zeros_like(q))
    slopes = [slopes_ref[g * heads + c // 2] for c in range(CHAINS)]
    vt_rows = [pl.ds((c // 2) * B_V_DIM, B_V_DIM) for c in range(CHAINS)]
    _attend_tile(i, k_ref, vt_ref, qm_scr, bias_scr, m_scr, l_scr, acc_scr, slopes, vt_rows, lambda c, j: None)
    lam = lam_ref[0]
    for h in range(heads):
        o1, o2 = [(acc_scr[2 * h + mm] * (1.0 / l_scr[2 * h + mm])).T for mm in range(2)]
        o = _diff_finish(o1, o2, lam, subw_ref[...], lam_init)
        o_ref[:, h * B_V_DIM:(h + 1) * B_V_DIM] = o.astype(o_ref.dtype)


def _diff_prompt(q_bf, k_bf, vt_bf, slopes, lam, subw, *, bsz, seq, lam_init):
    nq, q_spec, k_spec, vt_spec = _attn_specs(bsz, seq)
    return pl.pallas_call(
        functools.partial(_diff_prompt_kernel, lam_init=lam_init),
        grid=(bsz, WIDTH // GROUP_LANES, nq),
        in_specs=[_smem_spec(), _smem_spec(), q_spec, k_spec, vt_spec,
                  pl.BlockSpec((1, B_V_DIM), lambda b, g, i: (0, 0))],
        out_specs=q_spec,
        out_shape=jax.ShapeDtypeStruct((bsz * seq, WIDTH), BF16),
        scratch_shapes=[pltpu.VMEM((CHAINS, MOBA_BLOCK, GROUP_LANES), BF16)] + _attn_scratch(MOBA_BLOCK, B_V_DIM),
        compiler_params=pltpu.CompilerParams(dimension_semantics=("arbitrary",) * 3, vmem_limit_bytes=VMEM_LIMIT),
        name="diff_prompt",
    )(slopes, lam, q_bf, k_bf, vt_bf, subw)


def _head_rows(x, n_new):
    chunks = WIDTH // HEAD_LANES
    tiled = jnp.concatenate([x] * chunks, axis=0)
    r = lax.broadcasted_iota(jnp.int32, (chunks * n_new, WIDTH), 0) // n_new
    c = lax.broadcasted_iota(jnp.int32, (chunks * n_new, WIDTH), 1) // HEAD_LANES
    return jnp.where(r == c, tiled, 0.0).astype(BF16)


def _map_rows(x, n_new):
    lane_map = lax.broadcasted_iota(jnp.int32, (n_new, B_V_DIM), 1) // HEAD_LANES
    pieces = []
    for h in range(B_HEADS):
        xh = x[:, h * B_V_DIM:(h + 1) * B_V_DIM]
        for mm in range(2):
            pieces.append(jnp.where(lane_map == mm, xh, 0.0))
    return jnp.concatenate(pieces, axis=0).astype(BF16)


def _head_diag(full, n_new):
    lane_head = lax.broadcasted_iota(jnp.int32, (n_new, WIDTH), 1) // HEAD_LANES
    out = jnp.zeros((n_new, WIDTH), F32)
    for h in range(WIDTH // HEAD_LANES):
        out = out + jnp.where(lane_head == h, full[h * n_new:(h + 1) * n_new, :], 0.0)
    return out


def _col(x):
    return x[:, :1]


def _rep(x, rows):
    return jnp.broadcast_to(x, (rows, LANES))


def _sample_attn_kernel(pt_ref, lam_ref, qa_ref, qb_ref, kan_ref, van_ref, kbn_ref, vbn_ref, subw_ref,
                        sla_ref, slb_ref, bias_a_ref, bias_b_ref, *refs, nblk, n_new, lam_init):
    del pt_ref
    pps = SAMPLE_PAGES_PER_STEP
    ak, av, bk, bv = (refs[n * pps:(n + 1) * pps] for n in range(4))
    (oa_ref, ob_ref, qa_scr, qb_scr, qd_scr, gate_scr, mst_scr, lst_scr, acc_scr,
     dm_scr, dl_scr, dacc_scr) = refs[4 * pps:]
    step = pl.program_id(1)
    rows = qa_scr.shape[0]
    past_len = nblk * MOBA_BLOCK
    blocks_per_step = pps // 2
    lane = lax.broadcasted_iota(jnp.int32, (rows, LANES), 1)

    @pl.when(step == 0)
    def _():
        qa_scr[...] = _head_rows(qa_ref[...], n_new)
        qb_scr[...] = _head_rows(qb_ref[...], n_new)
        qd_scr[...] = _map_rows(qb_ref[...], n_new)
        gate_scr[...] = jnp.full(gate_scr.shape, NEG_INF, F32)
        mst_scr[...] = jnp.full(mst_scr.shape, NEG_INF, F32)
        lst_scr[...] = jnp.zeros(lst_scr.shape, F32)
        dm_scr[...] = jnp.full(dm_scr.shape, NEG_INF, F32)
        dl_scr[...] = jnp.zeros(dl_scr.shape, F32)
        dacc_scr[...] = jnp.zeros(dacc_scr.shape, F32)

    sla = sla_ref[...]
    slb = slb_ref[...]
    qa_rows = qa_scr[...]
    qd_rows = qd_scr[...]
    for b in range(blocks_per_step):
        jb = step * blocks_per_step + b
        blk_off = (jb * MOBA_BLOCK - past_len).astype(F32)
        here = lane == jb

        s_raw = jnp.concatenate([_dot(qa_rows, ak[2 * b + t][...].astype(BF16)) for t in range(2)], axis=1)
        gate = jnp.sum(s_raw, axis=1, keepdims=True) * (1.0 / MOBA_BLOCK)
        s = s_raw + bias_a_ref[...]
        m_loc = jnp.max(s, axis=1, keepdims=True)
        p = jnp.exp2(s - m_loc)
        l_loc = jnp.sum(p, axis=1, keepdims=True)
        pb = p.astype(BF16)
        acc_scr[jb] = sum(_dot_nt(pb[:, t * LANES:(t + 1) * LANES], av[2 * b + t][...].astype(BF16))
                          for t in range(2))
        gate_scr[...] = jnp.where(here, _rep(gate, rows), gate_scr[...])
        mst_scr[...] = jnp.where(here, _rep(m_loc, rows) + sla * blk_off, mst_scr[...])
        lst_scr[...] = jnp.where(here, _rep(l_loc, rows), lst_scr[...])

        s = jnp.concatenate([_dot_nt(qd_rows, bk[2 * b + t][...].astype(BF16)) for t in range(2)], axis=1)
        s = s + bias_b_ref[...]
        cj = _col(slb) * blk_off
        m_old = _col(dm_scr[...])
        m_new = jnp.maximum(m_old, jnp.max(s, axis=1, keepdims=True) + cj)
        p = jnp.exp2(s - (m_new - cj))
        alpha = jnp.exp2(m_old - m_new)
        dl_scr[...] = _rep(alpha * _col(dl_scr[...]) + jnp.sum(p, axis=1, keepdims=True), rows)
        pb = p.astype(BF16)
        half = pb.shape[1] // 2
        dacc_scr[...] = alpha * dacc_scr[...] + sum(
            _dot(pb[:, t * half:(t + 1) * half], bv[2 * b + t][...].astype(BF16)) for t in range(2))
        dm_scr[...] = _rep(m_new, rows)

    @pl.when(step == pl.num_programs(1) - 1)
    def _():
        pad = jnp.zeros((LANES - n_new, WIDTH), F32)
        qry = lax.broadcasted_iota(jnp.int32, (rows, LANES), 0) % n_new
        causal = lane <= qry
        new_off = lane.astype(F32)

        def new_block(q_rows, k_new, v_new, slope):
            kn = jnp.concatenate([k_new, pad], axis=0).astype(BF16)
            vn = jnp.concatenate([v_new, pad], axis=0).astype(BF16)
            sn = _dot_nt(q_rows, kn) + slope * new_off
            return jnp.where(causal, sn, NEG_INF), vn

        sn, vn = new_block(qa_rows, kan_ref[...], van_ref[...], sla)
        m_own = jnp.max(sn, axis=1, keepdims=True)
        pn = jnp.exp2(sn - m_own)
        l_own = jnp.sum(pn, axis=1, keepdims=True)
        acc_own = _dot(pn.astype(BF16), vn)
        gate = gate_scr[...]
        cnt = jnp.zeros((rows, LANES), jnp.int32)
        for jp in range(nblk):
            gj = gate[:, jp:jp + 1]
            beats = (gj > gate) | ((gj == gate) & (lane > jp))
            cnt = cnt + beats.astype(jnp.int32)
        sel = (cnt < MOBA_TOPK) & (lane < nblk)
        mst = mst_scr[...]
        m_all = jnp.maximum(m_own, jnp.max(jnp.where(sel, mst, NEG_INF), axis=1, keepdims=True))
        w = jnp.where(sel, jnp.exp2(mst - m_all), 0.0)
        w_own = jnp.exp2(m_own - m_all)
        l_all = w_own * l_own + jnp.sum(w * lst_scr[...], axis=1, keepdims=True)
        acc = w_own * acc_own
        for jb in range(nblk):
            acc = acc + w[:, jb:jb + 1] * acc_scr[jb]
        oa_ref[...] = _head_diag(acc * (1.0 / l_all), n_new)

        sn, vn = new_block(qb_scr[...], kbn_ref[...], vbn_ref[...], slb)
        m_old = _col(dm_scr[...])
        m_new = jnp.maximum(m_old, jnp.max(sn, axis=1, keepdims=True))
        pn = jnp.exp2(sn - m_new)
        alpha = jnp.exp2(m_old - m_new)
        l_d = alpha * _col(dl_scr[...]) + jnp.sum(pn, axis=1, keepdims=True)
        new_full = _dot(pn.astype(BF16), vn)
        per_head = rows // B_HEADS
        new_acc = jnp.concatenate([new_full[h * per_head:(h + 1) * per_head, h * B_V_DIM:(h + 1) * B_V_DIM]
                                   for h in range(B_HEADS)], axis=0)
        full = (alpha * dacc_scr[...] + new_acc) * (1.0 / l_d)
        lam = lam_ref[0]
        for h in range(B_HEADS):
            o1 = full[h * per_head:h * per_head + n_new, :]
            o2 = full[h * per_head + n_new:(h + 1) * per_head, :]
            ob_ref[:, h * B_V_DIM:(h + 1) * B_V_DIM] = _diff_finish(o1, o2, lam, subw_ref[...], lam_init)


def _sample_attention(qa, ka, va, qb, kb, vb, pool_ak, pool_av, pool_bk, pool_bv, page_table,
                      slopes_a, slopes_b, lam, subw, *, lam_init):
    n_seq, n_pages = page_table.shape
    n_new = qa.shape[0] // n_seq
    n_pool, page = pool_ak.shape[:2]
    pps = SAMPLE_PAGES_PER_STEP
    assert page == LANES and 2 * page == MOBA_BLOCK and n_pages % pps == 0 and n_new <= LANES
    nblk = n_pages // 2
    assert nblk <= LANES
    rows = (WIDTH // HEAD_LANES) * n_new
    moba_pages = [jnp.transpose(p, (0, 2, 3, 1)).reshape(n_pool, WIDTH, page) for p in (pool_ak, pool_av)]
    diff_pages = [p.reshape(n_pool, page * B_HEADS, B_V_DIM) for p in (pool_bk, pool_bv)]
    per_seq = [x.reshape(n_seq, n_new, WIDTH) for x in (qa, qb, ka, va, kb, vb)]

    row_slope_a = np.repeat(slopes_a, n_new)[:, None]
    row_slope_b = np.repeat(slopes_b, 2 * n_new)[:, None]
    row_head_b = np.repeat(np.arange(B_HEADS), 2 * n_new)[:, None]
    key_a = np.arange(MOBA_BLOCK)[None, :]
    col = np.arange(2 * page * B_HEADS)[None, :]
    bias_a = row_slope_a * key_a
    bias_b = np.where(col % B_HEADS == row_head_b, row_slope_b * (col // B_HEADS), NEG_INF)
    consts = [jnp.asarray(np.broadcast_to(row_slope_a, (rows, LANES)), F32),
              jnp.asarray(np.broadcast_to(row_slope_b, (rows, LANES)), F32),
              jnp.asarray(bias_a, F32), jnp.asarray(bias_b, F32)]

    seq_spec = pl.BlockSpec((None, n_new, WIDTH), lambda s, j, pt: (s, 0, 0))
    const2 = lambda s, j, pt: (0, 0)

    def page_spec(which, shape):
        return pl.BlockSpec((None,) + shape, lambda s, j, pt: (pt[s * n_pages + pps * j + which], 0, 0))

    cache_specs, cache_args = [], []
    for pool in moba_pages + diff_pages:
        for which in range(pps):
            cache_specs.append(page_spec(which, pool.shape[1:]))
            cache_args.append(pool)
    stat = pltpu.VMEM((rows, LANES), F32)
    grid_spec = pltpu.PrefetchScalarGridSpec(
        num_scalar_prefetch=1,
        grid=(n_seq, n_pages // pps),
        in_specs=[_smem_spec()] + [seq_spec] * 6 + [pl.BlockSpec((1, B_V_DIM), const2)]
                 + [pl.BlockSpec(c.shape, const2) for c in consts] + cache_specs,
        out_specs=[seq_spec, seq_spec],
        scratch_shapes=[pltpu.VMEM((rows, WIDTH), BF16), pltpu.VMEM((rows, WIDTH), BF16),
                        pltpu.VMEM((rows, B_V_DIM), BF16), stat, stat, stat,
                        pltpu.VMEM((nblk, rows, WIDTH), F32), stat, stat, pltpu.VMEM((rows, B_V_DIM), F32)])
    o_a, o_b = pl.pallas_call(
        functools.partial(_sample_attn_kernel, nblk=nblk, n_new=n_new, lam_init=lam_init),
        grid_spec=grid_spec,
        out_shape=[jax.ShapeDtypeStruct((n_seq, n_new, WIDTH), F32)] * 2,
        compiler_params=pltpu.CompilerParams(dimension_semantics=("arbitrary", "arbitrary"),
                                             vmem_limit_bytes=VMEM_LIMIT),
        name="sample_attention",
    )(page_table.reshape(-1), lam, *per_seq, subw, *consts, *cache_args)
    return o_a.reshape(n_seq * n_new, WIDTH), o_b.reshape(n_seq * n_new, WIDTH)


def _route(logits):
    lane_i = lax.broadcasted_iota(jnp.int32, logits.shape, 1)
    lane = lane_i.astype(F32)
    lane_grp = (lane_i // EXPERTS_PER_GROUP).astype(F32)

    def top(vals):
        v = jnp.max(vals, axis=1, keepdims=True)
        idx = jnp.min(jnp.where(vals == v, lane, float(LANES)), axis=1, keepdims=True)
        return v, idx

    is_grp = (lane_i >= N_EXPERTS) & (lane_i < N_EXPERTS + N_GROUPS)
    lg = jnp.where(is_grp, logits, NEG_INF)
    g_max, g_idx = top(lg)
    p_grp = 1.0 / jnp.sum(jnp.exp(lg - g_max), axis=1, keepdims=True)
    g_sel = g_idx - float(N_EXPERTS)
    in_grp = (lane_i < N_EXPERTS) & (lane_grp == g_sel)
    le = jnp.where(in_grp, logits, NEG_INF)
    v1, i1 = top(le)
    v2, i2 = top(jnp.where(lane == i1, NEG_INF, le))
    e2 = jnp.exp(v2 - v1)
    w1 = 1.0 / (1.0 + e2)
    w2 = e2 / (1.0 + e2)
    return jnp.where(lane == i1, w1 * p_grp, 0.0) + jnp.where(lane == i2, w2 * p_grp, 0.0)


def _merge_kernel(x_ref, oa_ref, ob_ref, g_ref, wpa_ref, wpb_ref, wo_ref, lng_ref, lnb_ref, wr_ref, br_ref,
                  h_ref, comb_ref, *, alpha):
    d = x_ref.shape[1]
    ya = _dot(oa_ref[...].astype(BF16), wpa_ref[...])
    yb = _dot(ob_ref[...].astype(BF16), wpb_ref[...])
    y = g_ref[:, :d] * ya + g_ref[:, d:] * yb
    mix = _dot(y.astype(BF16), wo_ref[...])
    h = _layer_norm(alpha * x_ref[...] + mix, lng_ref[...], lnb_ref[...])
    h_ref[...] = h
    logits = _dot(h.astype(BF16), wr_ref[...]) + br_ref[...]
    comb_ref[...] = _route(logits)


def _merge(x, o_a, o_b, gates, wpa, wpb, wo, ln_g, ln_b, w_router, b_router, *, alpha, name):
    t, d = x.shape
    tm = 256
    assert t % tm == 0
    row = lambda i: (i, 0)
    const = lambda i: (0, 0)
    return pl.pallas_call(
        functools.partial(_merge_kernel, alpha=alpha),
        grid=(t // tm,),
        in_specs=[pl.BlockSpec((tm, d), row), pl.BlockSpec((tm, WIDTH), row), pl.BlockSpec((tm, WIDTH), row),
                  pl.BlockSpec((tm, 2 * d), row),
                  pl.BlockSpec((WIDTH, d), const), pl.BlockSpec((WIDTH, d), const), pl.BlockSpec((d, d), const),
                  pl.BlockSpec((1, d), const), pl.BlockSpec((1, d), const),
                  pl.BlockSpec((d, LANES), const), pl.BlockSpec((1, LANES), const)],
        out_specs=[pl.BlockSpec((tm, d), row), pl.BlockSpec((tm, LANES), row)],
        out_shape=[jax.ShapeDtypeStruct((t, d), F32), jax.ShapeDtypeStruct((t, LANES), F32)],
        compiler_params=pltpu.CompilerParams(dimension_semantics=("arbitrary",), vmem_limit_bytes=VMEM_LIMIT),
        name=name,
    )(x, o_a, o_b, gates, wpa, wpb, wo, ln_g, ln_b, w_router, b_router)


def _moe_kernel(h_ref, comb_ref, wgu_ref, wd_ref, lng_ref, lnb_ref, o_ref, hb_scr, acc_scr, *, alpha):
    e = pl.program_id(1)
    f = wd_ref.shape[0]

    @pl.when(e == 0)
    def _():
        hb_scr[...] = h_ref[...].astype(BF16)
        acc_scr[...] = jnp.zeros(acc_scr.shape, F32)

    gu = _dot(hb_scr[...], wgu_ref[...])
    gate, up = gu[:, :f], gu[:, f:]
    comb = comb_ref[...]
    lane = lax.broadcasted_iota(jnp.int32, comb.shape, 1)
    c = jnp.sum(jnp.where(lane == e, comb, 0.0), axis=1, keepdims=True)
    act = gate / (1.0 + jnp.exp(-gate)) * up * c
    acc_scr[...] += _dot(act.astype(BF16), wd_ref[...])

    @pl.when(e == pl.num_programs(1) - 1)
    def _():
        o_ref[...] = _layer_norm(alpha * h_ref[...] + acc_scr[...], lng_ref[...], lnb_ref[...])


def _moe(h, comb, wgu, wd, ln_g, ln_b, *, alpha, name):
    t, d = h.shape
    tm = next(c for c in (1024, 512, 256) if t % c == 0)
    n_e, _, f2 = wgu.shape
    row = lambda i, e: (i, 0)
    const = lambda i, e: (0, 0)
    return pl.pallas_call(
        functools.partial(_moe_kernel, alpha=alpha),
        grid=(t // tm, n_e),
        in_specs=[pl.BlockSpec((tm, d), row), pl.BlockSpec((tm, LANES), row),
                  pl.BlockSpec((None, d, f2), lambda i, e: (e, 0, 0)),
                  pl.BlockSpec((None, f2 // 2, d), lambda i, e: (e, 0, 0)),
                  pl.BlockSpec((1, d), const), pl.BlockSpec((1, d), const)],
        out_specs=pl.BlockSpec((tm, d), row),
        out_shape=jax.ShapeDtypeStruct((t, d), F32),
        scratch_shapes=[pltpu.VMEM((tm, d), BF16), pltpu.VMEM((tm, d), F32)],
        compiler_params=pltpu.CompilerParams(dimension_semantics=("arbitrary", "arbitrary"),
                                             vmem_limit_bytes=VMEM_LIMIT),
        name=name,
    )(h, comb, wgu, wd, ln_g, ln_b)


def kernel(x_prompt, x_sample, cache_moba_k, cache_moba_v, cache_diff_k, cache_diff_v, page_table, w_in, b_gate, lambda_q1, lambda_k1, lambda_q2, lambda_k2, subln_w, w_proj_a, w_proj_b, w_out, ln1_g, ln1_b, ln2_g, ln2_b, w_route_group, b_route_group, w_route_expert, b_route_expert, w_expert_gate, w_expert_up, w_expert_down):
    depth = w_in.shape[0]
    bsz, seq, d = x_prompt.shape
    n_seq, n_new, _ = x_sample.shape
    alpha = (2 * depth) ** 0.25
    slopes_a = _alibi_slopes(A_HEADS) * LOG2E
    slopes_b = _alibi_slopes(B_HEADS) * LOG2E
    slopes_a_smem = jnp.asarray(slopes_a, F32)
    slopes_b_smem = jnp.asarray(slopes_b, F32)
    h_p = x_prompt.reshape(bsz * seq, d)
    h_s = x_sample.reshape(n_seq * n_new, d)
    rows_p, rows_s = [], []
    for l in range(depth):
        lam_init = _lambda_init(l)
        lam = (jnp.exp(jnp.sum(lambda_q1[l].astype(F32) * lambda_k1[l].astype(F32)))
               - jnp.exp(jnp.sum(lambda_q2[l].astype(F32) * lambda_k2[l].astype(F32))) + lam_init).reshape(1)
        w_bf = w_in[l].astype(BF16)
        w_t = [w_bf[:, slot * WIDTH:(slot + 1) * WIDTH].T for slot in (1, 2, 5)]
        subw = subln_w[l].reshape(1, B_V_DIM)
        wpa, wpb, wo = w_proj_a[l].astype(BF16), w_proj_b[l].astype(BF16), w_out[l].astype(BF16)
        n_route = N_EXPERTS + N_GROUPS
        w_router = jnp.pad(jnp.concatenate([w_route_expert[l], w_route_group[l]], axis=1),
                           ((0, 0), (0, LANES - n_route))).astype(BF16)
        b_router = jnp.pad(jnp.concatenate([b_route_expert[l], b_route_group[l]]), (0, LANES - n_route)).reshape(1, LANES)
        wgu = jnp.concatenate([w_expert_gate[l], w_expert_up[l]], axis=2).astype(BF16)
        wd = w_expert_down[l].astype(BF16)
        ln1 = (ln1_g[l].reshape(1, d), ln1_b[l].reshape(1, d))
        ln2 = (ln2_g[l].reshape(1, d), ln2_b[l].reshape(1, d))

        (qa, qb, kb, vb, gates, ka_bf, kb_bf, kat, vat, vat_bf, vbt_bf, kmean) = _in_projection(
            h_p, w_bf, b_gate[l], w_t, prompt=True, bsz=bsz)
        o_a = _moba_prompt(qa, ka_bf, vat_bf, kmean.reshape(bsz, seq // MOBA_BLOCK, WIDTH), slopes_a_smem,
                           bsz=bsz, seq=seq)
        o_b = _diff_prompt(qb, kb_bf, vbt_bf, slopes_b_smem, lam, subw, bsz=bsz, seq=seq, lam_init=lam_init)
        h1, comb = _merge(h_p, o_a, o_b, gates, wpa, wpb, wo, *ln1, w_router, b_router, alpha=alpha,
                          name="merge_prompt")
        h_p = _moe(h1, comb, wgu, wd, *ln2, alpha=alpha, name="moe_prompt")
        from_feat = lambda xt: jnp.transpose(xt.reshape(bsz, A_HEADS, A_HEAD_DIM, seq), (0, 3, 1, 2))
        rows_p.append((from_feat(kat), from_feat(vat),
                       kb.reshape(bsz, seq, B_HEADS, 2 * B_HEAD_DIM), vb.reshape(bsz, seq, B_HEADS, B_V_DIM)))

        qa, ka, va, qb, kb, vb, gates = _in_projection(h_s, w_bf, b_gate[l], None, prompt=False)
        o_a, o_b = _sample_attention(qa, ka, va, qb, kb, vb, cache_moba_k[l], cache_moba_v[l], cache_diff_k[l],
                                     cache_diff_v[l], page_table, slopes_a, slopes_b, lam, subw, lam_init=lam_init)
        h1, comb = _merge(h_s, o_a, o_b, gates, wpa, wpb, wo, *ln1, w_router, b_router, alpha=alpha,
                          name="merge_sample")
        h_s = _moe(h1, comb, wgu, wd, *ln2, alpha=alpha, name="moe_sample")
        rows_s.append((ka.reshape(n_seq, n_new, A_HEADS, A_HEAD_DIM), va.reshape(n_seq, n_new, A_HEADS, A_HEAD_DIM),
                       kb.reshape(n_seq, n_new, B_HEADS, 2 * B_HEAD_DIM), vb.reshape(n_seq, n_new, B_HEADS, B_V_DIM)))

    stack = lambda rows, k: jnp.stack([r[k] for r in rows])
    return (h_p.reshape(bsz, seq, d), h_s.reshape(n_seq, n_new, d),
            stack(rows_p, 0), stack(rows_p, 1), stack(rows_p, 2), stack(rows_p, 3),
            stack(rows_s, 0), stack(rows_s, 1), stack(rows_s, 2), stack(rows_s, 3))
```

```python
import functools
import math

import numpy as np
import jax
import jax.numpy as jnp
from jax import lax
from jax.experimental import pallas as pl
from jax.experimental.pallas import tpu as pltpu

F32 = jnp.float32
BF16 = jnp.bfloat16

A_HEADS = 8
A_HEAD_DIM = 64
B_HEADS = 4
B_HEAD_DIM = 64
B_V_DIM = 2 * B_HEAD_DIM
WIDTH = A_HEADS * A_HEAD_DIM
MOBA_BLOCK = 256
MOBA_TOPK = 3
N_GROUPS = 4
EXPERTS_PER_GROUP = 4
N_EXPERTS = N_GROUPS * EXPERTS_PER_GROUP
LN_EPS = 1e-5
LANES = 128
HEAD_LANES = 64
GROUP_LANES = 256
CHAINS = GROUP_LANES // HEAD_LANES
NEG_INF = float("-inf")
POS_INF = float("inf")
LOG2E = math.log2(math.e)
VMEM_LIMIT = 56 * 1024 * 1024
SAMPLE_PAGES_PER_STEP = 8

_NT = (((1,), (1,)), ((), ()))


def _lambda_init(layer):
    return 0.8 - 0.6 * math.exp(-0.3 * layer)


def _alibi_slopes(n_heads):
    return np.power(2.0, -8.0 * np.arange(1, n_heads + 1) / n_heads)


def _smem_spec():
    return pl.BlockSpec(memory_space=pltpu.SMEM)


def _layer_norm(x, g, b):
    mu = jnp.mean(x, axis=-1, keepdims=True)
    xc = x - mu
    var = jnp.mean(xc * xc, axis=-1, keepdims=True)
    return xc * lax.rsqrt(var + LN_EPS) * g + b


def _dot(a, b):
    return jnp.dot(a, b, preferred_element_type=F32)


def _dot_nt(a, b):
    return lax.dot_general(a, b, _NT, preferred_element_type=F32)


def _inproj_kernel(x_ref, w_ref, bg_ref, *refs, prompt):
    if prompt:
        (wkat_ref, wvat_ref, wvbt_ref, qa_ref, qb_ref, kb_ref, vb_ref, g_ref,
         kab_ref, kbb_ref, kat_ref, vat_ref, vatb_ref, vbtb_ref, kmean_ref) = refs
    else:
        qa_ref, ka_ref, va_ref, qb_ref, kb_ref, vb_ref, g_ref = refs
    xb = x_ref[...].astype(BF16)

    def proj(slot):
        return _dot(xb, w_ref[:, slot * WIDTH:(slot + 1) * WIDTH])

    q_scale = A_HEAD_DIM ** -0.5 * LOG2E
    qa_ref[...] = (proj(0) * q_scale).astype(qa_ref.dtype)
    qb_ref[...] = (proj(3) * q_scale).astype(qb_ref.dtype)
    ka = proj(1)
    kb = proj(4)
    kb_ref[...] = kb
    vb_ref[...] = proj(5)
    z = _dot(xb, w_ref[:, 6 * WIDTH:]) + bg_ref[...]
    g_ref[...] = 1.0 / (1.0 + jnp.exp(-z))
    if prompt:
        kab_ref[...] = ka.astype(BF16)
        kbb_ref[...] = kb.astype(BF16)
        kmean_ref[...] = jnp.mean(ka, axis=0, keepdims=True)
        kat_ref[...] = _dot_nt(wkat_ref[...], xb)
        vat = _dot_nt(wvat_ref[...], xb)
        vat_ref[...] = vat
        vatb_ref[...] = vat.astype(BF16)
        vbtb_ref[...] = _dot_nt(wvbt_ref[...], xb).astype(BF16)
    else:
        ka_ref[...] = ka
        va_ref[...] = proj(2)


def _in_projection(x, w_bf, b_gate, w_t, *, prompt, bsz=1):
    t, d = x.shape
    tm = MOBA_BLOCK
    assert t % (bsz * tm) == 0
    seq = t // bsz
    nt = seq // tm
    n_in = w_bf.shape[1]
    row = lambda i: (i, 0)
    const = lambda i: (0, 0)
    rows_f32 = jax.ShapeDtypeStruct((t, WIDTH), F32)
    rows_bf = jax.ShapeDtypeStruct((t, WIDTH), BF16)
    row_spec = pl.BlockSpec((tm, WIDTH), row)
    in_specs = [pl.BlockSpec((tm, d), row), pl.BlockSpec((d, n_in), const), pl.BlockSpec((1, 2 * d), const)]
    args = [x, w_bf, b_gate.reshape(1, 2 * d)]
    gate_shape = jax.ShapeDtypeStruct((t, 2 * d), F32)
    gate_spec = pl.BlockSpec((tm, 2 * d), row)
    if prompt:
        in_specs += [pl.BlockSpec((WIDTH, d), const)] * 3
        args += list(w_t)
        feat_f32 = jax.ShapeDtypeStruct((bsz, WIDTH, seq), F32)
        feat_bf = jax.ShapeDtypeStruct((bsz, WIDTH, seq), BF16)
        feat_spec = pl.BlockSpec((None, WIDTH, tm), lambda i: (i // nt, 0, i % nt))
        out_shape = [rows_bf, rows_bf, rows_f32, rows_f32, gate_shape, rows_bf, rows_bf,
                     feat_f32, feat_f32, feat_bf, feat_bf, jax.ShapeDtypeStruct((t // tm, 1, WIDTH), F32)]
        out_specs = [row_spec] * 4 + [gate_spec] + [row_spec] * 2 + [feat_spec] * 4
        out_specs += [pl.BlockSpec((None, 1, WIDTH), lambda i: (i, 0, 0))]
    else:
        out_shape = [rows_f32] * 6 + [gate_shape]
        out_specs = [row_spec] * 6 + [gate_spec]
    return pl.pallas_call(
        functools.partial(_inproj_kernel, prompt=prompt),
        grid=(t // tm,),
        in_specs=in_specs, out_specs=out_specs, out_shape=out_shape,
        compiler_params=pltpu.CompilerParams(dimension_semantics=("arbitrary",), vmem_limit_bytes=VMEM_LIMIT),
        name="in_projection_prompt" if prompt else "in_projection_sample",
    )(*args)


def _attend_tile(i, k_ref, vt_ref, qm_scr, bias_scr, m_scr, l_scr, acc_scr, u_scr, p_scr, slopes, vt_rows, sel_row):
    tq = qm_scr.shape[1]
    key_row = lax.broadcasted_iota(jnp.int32, (MOBA_BLOCK, tq), 0)
    causal = key_row <= lax.broadcasted_iota(jnp.int32, (MOBA_BLOCK, tq), 1)
    key_off = key_row.astype(F32)
    for c in range(CHAINS):
        bias_scr[c] = slopes[c] * key_off
        m_scr[c] = jnp.full((1, tq), NEG_INF, F32)
        l_scr[c] = jnp.zeros((1, tq), F32)
        acc_scr[c] = jnp.zeros(acc_scr.shape[1:], F32)

    def scores(j, slot):
        kj = k_ref[pl.ds(pl.multiple_of(j * MOBA_BLOCK, MOBA_BLOCK), MOBA_BLOCK), :]
        for c in range(CHAINS):
            u_scr[slot * CHAINS + c] = _dot_nt(kj, qm_scr[c])

    def block(j, slot, own):
        start = pl.multiple_of(j * MOBA_BLOCK, MOBA_BLOCK)
        alphas = []
        for c in range(CHAINS):
            u = u_scr[slot * CHAINS + c] + bias_scr[c]
            m_old = m_scr[c]
            if own:
                u = jnp.where(causal, u, NEG_INF)
                m_new = jnp.maximum(m_old, jnp.max(u, axis=0, keepdims=True))
                shift = m_new
            else:
                cj = slopes[c] * ((j - i) * MOBA_BLOCK).astype(F32)
                m_blk = jnp.max(u, axis=0, keepdims=True) + cj
                pick = sel_row(c, j)
                if pick is not None:
                    m_blk = jnp.where(pick > 0.0, m_blk, NEG_INF)
                m_new = jnp.maximum(m_old, m_blk)
                shift = m_new - cj
                if pick is not None:
                    shift = jnp.where(pick > 0.0, shift, POS_INF)
            p = jnp.exp2(u - shift)
            alpha = jnp.exp2(m_old - m_new)
            m_scr[c] = m_new
            l_scr[c] = alpha * l_scr[c] + jnp.sum(p, axis=0, keepdims=True)
            p_scr[c] = p.astype(BF16)
            alphas.append(alpha)
        for c in range(CHAINS):
            vtj = vt_ref[vt_rows[c], pl.ds(start, MOBA_BLOCK)]
            acc_scr[c] = alphas[c] * acc_scr[c] + _dot(vtj, p_scr[c])

    scores(i, 0)
    scores(0, 1)
    block(i, 0, True)

    def pair(jj, carry):
        j = 2 * jj
        scores(j + 1, 0)
        block(j, 1, False)
        scores(jnp.minimum(j + 2, i), 1)
        block(j + 1, 0, False)
        return carry

    lax.fori_loop(0, i // 2, pair, 0)

    @pl.when(i % 2 == 1)
    def _():
        block(i - 1, 1, False)


def _moba_prompt_kernel(slopes_ref, q_ref, k_ref, vt_ref, kmean_ref, o_ref,
                        qm_scr, sel_scr, bias_scr, m_scr, l_scr, acc_scr, u_scr, p_scr, *, nb):
    g = pl.program_id(1)
    i = pl.program_id(2)
    tq = q_ref.shape[0]
    q = q_ref[...]
    lane_head = lax.broadcasted_iota(jnp.int32, (1, GROUP_LANES), 1) // HEAD_LANES
    km = kmean_ref[...].astype(BF16)
    blk = lax.broadcasted_iota(jnp.int32, (nb, tq), 0)
    past = blk < i
    for h in range(CHAINS):
        qm = jnp.where(lane_head == h, q, jnp.zeros_like(q))
        qm_scr[h] = qm
        gate = jnp.where(past, _dot_nt(km, qm), NEG_INF)
        cnt = jnp.zeros((nb, tq), jnp.int32)
        for jp in range(nb):
            gj = gate[jp:jp + 1, :]
            beats = (gj > gate) | ((gj == gate) & (blk > jp))
            cnt = cnt + beats.astype(jnp.int32)
        sel_scr[h] = jnp.where((cnt < MOBA_TOPK) & past, 1.0, 0.0).astype(F32)

    slopes = [slopes_ref[g * CHAINS + h] for h in range(CHAINS)]
    vt_rows = [pl.ds((h // 2) * LANES, LANES) for h in range(CHAINS)]
    _attend_tile(i, k_ref, vt_ref, qm_scr, bias_scr, m_scr, l_scr, acc_scr, u_scr, p_scr, slopes, vt_rows,
                 lambda c, j: sel_scr[c, pl.ds(j, 1), :])
    outs = []
    for h in range(CHAINS):
        half = (h % 2) * HEAD_LANES
        outs.append(acc_scr[h, half:half + HEAD_LANES, :] * (1.0 / l_scr[h]))
    o_ref[...] = jnp.concatenate(outs, axis=0).T.astype(o_ref.dtype)


def _attn_scratch(tq, acc_rows):
    return [pltpu.VMEM((CHAINS, MOBA_BLOCK, tq), F32), pltpu.VMEM((CHAINS, 1, tq), F32),
            pltpu.VMEM((CHAINS, 1, tq), F32), pltpu.VMEM((CHAINS, acc_rows, tq), F32),
            pltpu.VMEM((2 * CHAINS, MOBA_BLOCK, tq), F32), pltpu.VMEM((CHAINS, MOBA_BLOCK, tq), BF16)]


def _attn_specs(bsz, seq):
    nq = seq // MOBA_BLOCK
    q_spec = pl.BlockSpec((MOBA_BLOCK, GROUP_LANES), lambda b, g, i: (b * nq + i, g))
    k_spec = pl.BlockSpec((seq, GROUP_LANES), lambda b, g, i: (b, g))
    vt_spec = pl.BlockSpec((None, GROUP_LANES, seq), lambda b, g, i: (b, g, 0))
    return nq, q_spec, k_spec, vt_spec


def _moba_prompt(q_bf, k_bf, vt_bf, kmean, slopes, *, bsz, seq):
    nq, q_spec, k_spec, vt_spec = _attn_specs(bsz, seq)
    return pl.pallas_call(
        functools.partial(_moba_prompt_kernel, nb=nq),
        grid=(bsz, WIDTH // GROUP_LANES, nq),
        in_specs=[_smem_spec(), q_spec, k_spec, vt_spec,
                  pl.BlockSpec((None, nq, GROUP_LANES), lambda b, g, i: (b, 0, g))],
        out_specs=q_spec,
        out_shape=jax.ShapeDtypeStruct((bsz * seq, WIDTH), BF16),
        scratch_shapes=[pltpu.VMEM((CHAINS, MOBA_BLOCK, GROUP_LANES), BF16),
                        pltpu.VMEM((CHAINS, nq, MOBA_BLOCK), F32)] + _attn_scratch(MOBA_BLOCK, LANES),
        compiler_params=pltpu.CompilerParams(dimension_semantics=("arbitrary",) * 3, vmem_limit_bytes=VMEM_LIMIT),
        name="moba_prompt",
    )(slopes, q_bf, k_bf, vt_bf, kmean)


def _diff_finish(o1, o2, lam, subw, lam_init):
    o = o1 - lam * o2
    o = o * lax.rsqrt(jnp.mean(o * o, axis=-1, keepdims=True) + LN_EPS)
    return o * subw * (1.0 - lam_init)


def _diff_prompt_kernel(slopes_ref, lam_ref, q_ref, k_ref, vt_ref, subw_ref, o_ref,
                        qm_scr, bias_scr, m_scr, l_scr, acc_scr, u_scr, p_scr, *, lam_init):
    g = pl.program_id(1)
    i = pl.program_id(2)
    heads = CHAINS // 2
    q = q_ref[...]
    lane_map = lax.broadcasted_iota(jnp.int32, (1, GROUP_LANES), 1) // HEAD_LANES
    for c in range(CHAINS):
        qm_scr[c] = jnp.where(lane_map == c, q, jnp.zeros_like(q))
    slopes = [slopes_ref[g * heads + c // 2] for c in range(CHAINS)]
    vt_rows = [pl.ds((c // 2) * B_V_DIM, B_V_DIM) for c in range(CHAINS)]
    _attend_tile(i, k_ref, vt_ref, qm_scr, bias_scr, m_scr, l_scr, acc_scr, u_scr, p_scr, slopes, vt_rows, lambda c, j: None)
    lam = lam_ref[0]
    for h in range(heads):
        o1, o2 = [(acc_scr[2 * h + mm] * (1.0 / l_scr[2 * h + mm])).T for mm in range(2)]
        o = _diff_finish(o1, o2, lam, subw_ref[...], lam_init)
        o_ref[:, h * B_V_DIM:(h + 1) * B_V_DIM] = o.astype(o_ref.dtype)


def _diff_prompt(q_bf, k_bf, vt_bf, slopes, lam, subw, *, bsz, seq, lam_init):
    nq, q_spec, k_spec, vt_spec = _attn_specs(bsz, seq)
    return pl.pallas_call(
        functools.partial(_diff_prompt_kernel, lam_init=lam_init),
        grid=(bsz, WIDTH // GROUP_LANES, nq),
        in_specs=[_smem_spec(), _smem_spec(), q_spec, k_spec, vt_spec,
                  pl.BlockSpec((1, B_V_DIM), lambda b, g, i: (0, 0))],
        out_specs=q_spec,
        out_shape=jax.ShapeDtypeStruct((bsz * seq, WIDTH), BF16),
        scratch_shapes=[pltpu.VMEM((CHAINS, MOBA_BLOCK, GROUP_LANES), BF16)] + _attn_scratch(MOBA_BLOCK, B_V_DIM),
        compiler_params=pltpu.CompilerParams(dimension_semantics=("arbitrary",) * 3, vmem_limit_bytes=VMEM_LIMIT),
        name="diff_prompt",
    )(slopes, lam, q_bf, k_bf, vt_bf, subw)


def _head_rows(x, n_new):
    chunks = WIDTH // HEAD_LANES
    tiled = jnp.concatenate([x] * chunks, axis=0)
    r = lax.broadcasted_iota(jnp.int32, (chunks * n_new, WIDTH), 0) // n_new
    c = lax.broadcasted_iota(jnp.int32, (chunks * n_new, WIDTH), 1) // HEAD_LANES
    return jnp.where(r == c, tiled, 0.0).astype(BF16)


def _map_rows(x, n_new):
    lane_map = lax.broadcasted_iota(jnp.int32, (n_new, B_V_DIM), 1) // HEAD_LANES
    pieces = []
    for h in range(B_HEADS):
        xh = x[:, h * B_V_DIM:(h + 1) * B_V_DIM]
        for mm in range(2):
            pieces.append(jnp.where(lane_map == mm, xh, 0.0))
    return jnp.concatenate(pieces, axis=0).astype(BF16)


def _head_diag(full, n_new):
    lane_head = lax.broadcasted_iota(jnp.int32, (n_new, WIDTH), 1) // HEAD_LANES
    out = jnp.zeros((n_new, WIDTH), F32)
    for h in range(WIDTH // HEAD_LANES):
        out = out + jnp.where(lane_head == h, full[h * n_new:(h + 1) * n_new, :], 0.0)
    return out


def _col(x):
    return x[:, :1]


def _rep(x, rows):
    return jnp.broadcast_to(x, (rows, LANES))


def _sample_attn_kernel(pt_ref, lam_ref, qa_ref, qb_ref, kan_ref, van_ref, kbn_ref, vbn_ref, subw_ref,
                        sla_ref, slb_ref, bias_a_ref, bias_b_ref, *refs, nblk, n_new, lam_init):
    del pt_ref
    pps = SAMPLE_PAGES_PER_STEP
    ak, av, bk, bv = (refs[n * pps:(n + 1) * pps] for n in range(4))
    (oa_ref, ob_ref, qa_scr, qb_scr, qd_scr, gate_scr, mst_scr, lst_scr, acc_scr,
     dm_scr, dl_scr, dacc_scr, sd_scr, pd_scr) = refs[4 * pps:]
    step = pl.program_id(1)
    rows = qa_scr.shape[0]
    past_len = nblk * MOBA_BLOCK
    blocks_per_step = pps // 2
    lane = lax.broadcasted_iota(jnp.int32, (rows, LANES), 1)

    @pl.when(step == 0)
    def _():
        qa_scr[...] = _head_rows(qa_ref[...], n_new)
        qb_scr[...] = _head_rows(qb_ref[...], n_new)
        qd_scr[...] = _map_rows(qb_ref[...], n_new)
        gate_scr[...] = jnp.full(gate_scr.shape, NEG_INF, F32)
        mst_scr[...] = jnp.full(mst_scr.shape, NEG_INF, F32)
        lst_scr[...] = jnp.zeros(lst_scr.shape, F32)
        dm_scr[...] = jnp.full(dm_scr.shape, NEG_INF, F32)
        dl_scr[...] = jnp.zeros(dl_scr.shape, F32)
        dacc_scr[...] = jnp.zeros(dacc_scr.shape, F32)

    sla = sla_ref[...]
    slb = slb_ref[...]
    qa_rows = qa_scr[...]
    qd_rows = qd_scr[...]
    dcols = bk[0].shape[0]

    sa = [_dot(qa_rows, ak[t][...].astype(BF16)) for t in range(pps)]
    for t in range(pps):
        sd_scr[:, t * dcols:(t + 1) * dcols] = _dot_nt(qd_rows, bk[t][...].astype(BF16))

    pa = []
    for b in range(blocks_per_step):
        jb = step * blocks_per_step + b
        blk_off = (jb * MOBA_BLOCK - past_len).astype(F32)
        here = lane == jb
        s_raw = jnp.concatenate(sa[2 * b:2 * b + 2], axis=1)
        gate = jnp.sum(s_raw, axis=1, keepdims=True) * (1.0 / MOBA_BLOCK)
        s = s_raw + bias_a_ref[...]
        m_loc = jnp.max(s, axis=1, keepdims=True)
        p = jnp.exp2(s - m_loc)
        l_loc = jnp.sum(p, axis=1, keepdims=True)
        pa.append(p.astype(BF16))
        gate_scr[...] = jnp.where(here, _rep(gate, rows), gate_scr[...])
        mst_scr[...] = jnp.where(here, _rep(m_loc, rows) + sla * blk_off, mst_scr[...])
        lst_scr[...] = jnp.where(here, _rep(l_loc, rows), lst_scr[...])

    step_off = _col(slb) * (step * (pps * LANES) - past_len).astype(F32)
    m_old = _col(dm_scr[...])
    m_loc = jnp.full((rows, 1), NEG_INF, F32)
    for t in range(pps):
        cols = slice(t * dcols, (t + 1) * dcols)
        m_loc = jnp.maximum(m_loc, jnp.max(sd_scr[:, cols] + bias_b_ref[:, cols], axis=1, keepdims=True))
    m_new = jnp.maximum(m_old, m_loc + step_off)
    shift = m_new - step_off
    l_loc = jnp.zeros((rows, 1), F32)
    for t in range(pps):
        cols = slice(t * dcols, (t + 1) * dcols)
        p = jnp.exp2(sd_scr[:, cols] + bias_b_ref[:, cols] - shift)
        l_loc = l_loc + jnp.sum(p, axis=1, keepdims=True)
        pd_scr[:, cols] = p.astype(BF16)
    alpha = jnp.exp2(m_old - m_new)
    dl_scr[...] = _rep(alpha * _col(dl_scr[...]) + l_loc, rows)
    dm_scr[...] = _rep(m_new, rows)

    for b in range(blocks_per_step):
        acc_scr[step * blocks_per_step + b] = sum(
            _dot_nt(pa[b][:, t * LANES:(t + 1) * LANES], av[2 * b + t][...].astype(BF16)) for t in range(2))
    dacc_scr[...] = alpha * dacc_scr[...] + sum(
        _dot(pd_scr[:, t * dcols:(t + 1) * dcols], bv[t][...].astype(BF16)) for t in range(pps))

    @pl.when(step == pl.num_programs(1) - 1)
    def _():
        pad = jnp.zeros((LANES - n_new, WIDTH), F32)
        qry = lax.broadcasted_iota(jnp.int32, (rows, LANES), 0) % n_new
        causal = lane <= qry
        new_off = lane.astype(F32)

        def new_block(q_rows, k_new, v_new, slope):
            kn = jnp.concatenate([k_new, pad], axis=0).astype(BF16)
            vn = jnp.concatenate([v_new, pad], axis=0).astype(BF16)
            sn = _dot_nt(q_rows, kn) + slope * new_off
            return jnp.where(causal, sn, NEG_INF), vn

        sn, vn = new_block(qa_rows, kan_ref[...], van_ref[...], sla)
        m_own = jnp.max(sn, axis=1, keepdims=True)
        pn = jnp.exp2(sn - m_own)
        l_own = jnp.sum(pn, axis=1, keepdims=True)
        acc_own = _dot(pn.astype(BF16), vn)
        gate = gate_scr[...]
        cnt = jnp.zeros((rows, LANES), jnp.int32)
        for jp in range(nblk):
            gj = gate[:, jp:jp + 1]
            beats = (gj > gate) | ((gj == gate) & (lane > jp))
            cnt = cnt + beats.astype(jnp.int32)
        sel = (cnt < MOBA_TOPK) & (lane < nblk)
        mst = mst_scr[...]
        m_all = jnp.maximum(m_own, jnp.max(jnp.where(sel, mst, NEG_INF), axis=1, keepdims=True))
        w = jnp.where(sel, jnp.exp2(mst - m_all), 0.0)
        w_own = jnp.exp2(m_own - m_all)
        l_all = w_own * l_own + jnp.sum(w * lst_scr[...], axis=1, keepdims=True)
        acc = w_own * acc_own
        for jb in range(nblk):
            acc = acc + w[:, jb:jb + 1] * acc_scr[jb]
        oa_ref[...] = _head_diag(acc * (1.0 / l_all), n_new)

        sn, vn = new_block(qb_scr[...], kbn_ref[...], vbn_ref[...], slb)
        m_old = _col(dm_scr[...])
        m_new = jnp.maximum(m_old, jnp.max(sn, axis=1, keepdims=True))
        pn = jnp.exp2(sn - m_new)
        alpha = jnp.exp2(m_old - m_new)
        l_d = alpha * _col(dl_scr[...]) + jnp.sum(pn, axis=1, keepdims=True)
        new_full = _dot(pn.astype(BF16), vn)
        per_head = rows // B_HEADS
        new_acc = jnp.concatenate([new_full[h * per_head:(h + 1) * per_head, h * B_V_DIM:(h + 1) * B_V_DIM]
                                   for h in range(B_HEADS)], axis=0)
        full = (alpha * dacc_scr[...] + new_acc) * (1.0 / l_d)
        lam = lam_ref[0]
        for h in range(B_HEADS):
            o1 = full[h * per_head:h * per_head + n_new, :]
            o2 = full[h * per_head + n_new:(h + 1) * per_head, :]
            ob_ref[:, h * B_V_DIM:(h + 1) * B_V_DIM] = _diff_finish(o1, o2, lam, subw_ref[...], lam_init)


def _sample_attention(qa, ka, va, qb, kb, vb, pool_ak, pool_av, pool_bk, pool_bv, page_table,
                      slopes_a, slopes_b, lam, subw, *, lam_init):
    n_seq, n_pages = page_table.shape
    n_new = qa.shape[0] // n_seq
    n_pool, page = pool_ak.shape[:2]
    pps = SAMPLE_PAGES_PER_STEP
    assert page == LANES and 2 * page == MOBA_BLOCK and n_pages % pps == 0 and n_new <= LANES
    nblk = n_pages // 2
    assert nblk <= LANES
    rows = (WIDTH // HEAD_LANES) * n_new
    moba_pages = [jnp.transpose(p, (0, 2, 3, 1)).reshape(n_pool, WIDTH, page) for p in (pool_ak, pool_av)]
    diff_pages = [p.reshape(n_pool, page * B_HEADS, B_V_DIM) for p in (pool_bk, pool_bv)]
    per_seq = [x.reshape(n_seq, n_new, WIDTH) for x in (qa, qb, ka, va, kb, vb)]

    row_slope_a = np.repeat(slopes_a, n_new)[:, None]
    row_slope_b = np.repeat(slopes_b, 2 * n_new)[:, None]
    row_head_b = np.repeat(np.arange(B_HEADS), 2 * n_new)[:, None]
    key_a = np.arange(MOBA_BLOCK)[None, :]
    col = np.arange(pps * page * B_HEADS)[None, :]
    bias_a = row_slope_a * key_a
    bias_b = np.where(col % B_HEADS == row_head_b, row_slope_b * (col // B_HEADS), NEG_INF)
    consts = [jnp.asarray(np.broadcast_to(row_slope_a, (rows, LANES)), F32),
              jnp.asarray(np.broadcast_to(row_slope_b, (rows, LANES)), F32),
              jnp.asarray(bias_a, F32), jnp.asarray(bias_b, F32)]

    seq_spec = pl.BlockSpec((None, n_new, WIDTH), lambda s, j, pt: (s, 0, 0))
    const2 = lambda s, j, pt: (0, 0)

    def page_spec(which, shape):
        return pl.BlockSpec((None,) + shape, lambda s, j, pt: (pt[s * n_pages + pps * j + which], 0, 0))

    cache_specs, cache_args = [], []
    for pool in moba_pages + diff_pages:
        for which in range(pps):
            cache_specs.append(page_spec(which, pool.shape[1:]))
            cache_args.append(pool)
    stat = pltpu.VMEM((rows, LANES), F32)
    grid_spec = pltpu.PrefetchScalarGridSpec(
        num_scalar_prefetch=1,
        grid=(n_seq, n_pages // pps),
        in_specs=[_smem_spec()] + [seq_spec] * 6 + [pl.BlockSpec((1, B_V_DIM), const2)]
                 + [pl.BlockSpec(c.shape, const2) for c in consts] + cache_specs,
        out_specs=[seq_spec, seq_spec],
        scratch_shapes=[pltpu.VMEM((rows, WIDTH), BF16), pltpu.VMEM((rows, WIDTH), BF16),
                        pltpu.VMEM((rows, B_V_DIM), BF16), stat, stat, stat,
                        pltpu.VMEM((nblk, rows, WIDTH), F32), stat, stat, pltpu.VMEM((rows, B_V_DIM), F32),
                        pltpu.VMEM((rows, pps * page * B_HEADS), F32),
                        pltpu.VMEM((rows, pps * page * B_HEADS), BF16)])
    o_a, o_b = pl.pallas_call(
        functools.partial(_sample_attn_kernel, nblk=nblk, n_new=n_new, lam_init=lam_init),
        grid_spec=grid_spec,
        out_shape=[jax.ShapeDtypeStruct((n_seq, n_new, WIDTH), F32)] * 2,
        compiler_params=pltpu.CompilerParams(dimension_semantics=("arbitrary", "arbitrary"),
                                             vmem_limit_bytes=VMEM_LIMIT),
        name="sample_attention",
    )(page_table.reshape(-1), lam, *per_seq, subw, *consts, *cache_args)
    return o_a.reshape(n_seq * n_new, WIDTH), o_b.reshape(n_seq * n_new, WIDTH)


def _route(logits):
    lane_i = lax.broadcasted_iota(jnp.int32, logits.shape, 1)
    lane = lane_i.astype(F32)
    lane_grp = (lane_i // EXPERTS_PER_GROUP).astype(F32)

    def top(vals):
        v = jnp.max(vals, axis=1, keepdims=True)
        idx = jnp.min(jnp.where(vals == v, lane, float(LANES)), axis=1, keepdims=True)
        return v, idx

    is_grp = (lane_i >= N_EXPERTS) & (lane_i < N_EXPERTS + N_GROUPS)
    lg = jnp.where(is_grp, logits, NEG_INF)
    g_max, g_idx = top(lg)
    p_grp = 1.0 / jnp.sum(jnp.exp(lg - g_max), axis=1, keepdims=True)
    g_sel = g_idx - float(N_EXPERTS)
    in_grp = (lane_i < N_EXPERTS) & (lane_grp == g_sel)
    le = jnp.where(in_grp, logits, NEG_INF)
    v1, i1 = top(le)
    v2, i2 = top(jnp.where(lane == i1, NEG_INF, le))
    e2 = jnp.exp(v2 - v1)
    w1 = 1.0 / (1.0 + e2)
    w2 = e2 / (1.0 + e2)
    return jnp.where(lane == i1, w1 * p_grp, 0.0) + jnp.where(lane == i2, w2 * p_grp, 0.0)


def _merge_kernel(x_ref, oa_ref, ob_ref, g_ref, wpa_ref, wpb_ref, wo_ref, lng_ref, lnb_ref, wr_ref, br_ref,
                  h_ref, comb_ref, *, alpha):
    d = x_ref.shape[1]
    ya = _dot(oa_ref[...].astype(BF16), wpa_ref[...])
    yb = _dot(ob_ref[...].astype(BF16), wpb_ref[...])
    y = g_ref[:, :d] * ya + g_ref[:, d:] * yb
    mix = _dot(y.astype(BF16), wo_ref[...])
    h = _layer_norm(alpha * x_ref[...] + mix, lng_ref[...], lnb_ref[...])
    h_ref[...] = h
    logits = _dot(h.astype(BF16), wr_ref[...]) + br_ref[...]
    comb_ref[...] = _route(logits)


def _merge(x, o_a, o_b, gates, wpa, wpb, wo, ln_g, ln_b, w_router, b_router, *, alpha, name):
    t, d = x.shape
    tm = 256
    assert t % tm == 0
    row = lambda i: (i, 0)
    const = lambda i: (0, 0)
    return pl.pallas_call(
        functools.partial(_merge_kernel, alpha=alpha),
        grid=(t // tm,),
        in_specs=[pl.BlockSpec((tm, d), row), pl.BlockSpec((tm, WIDTH), row), pl.BlockSpec((tm, WIDTH), row),
                  pl.BlockSpec((tm, 2 * d), row),
                  pl.BlockSpec((WIDTH, d), const), pl.BlockSpec((WIDTH, d), const), pl.BlockSpec((d, d), const),
                  pl.BlockSpec((1, d), const), pl.BlockSpec((1, d), const),
                  pl.BlockSpec((d, LANES), const), pl.BlockSpec((1, LANES), const)],
        out_specs=[pl.BlockSpec((tm, d), row), pl.BlockSpec((tm, LANES), row)],
        out_shape=[jax.ShapeDtypeStruct((t, d), F32), jax.ShapeDtypeStruct((t, LANES), F32)],
        compiler_params=pltpu.CompilerParams(dimension_semantics=("arbitrary",), vmem_limit_bytes=VMEM_LIMIT),
        name=name,
    )(x, o_a, o_b, gates, wpa, wpb, wo, ln_g, ln_b, w_router, b_router)


def _moe_kernel(h_ref, comb_ref, wgu_ref, wd_ref, lng_ref, lnb_ref, o_ref, hb_scr, acc_scr, *, alpha):
    e = pl.program_id(1)
    f = wd_ref.shape[0]

    @pl.when(e == 0)
    def _():
        hb_scr[...] = h_ref[...].astype(BF16)
        acc_scr[...] = jnp.zeros(acc_scr.shape, F32)

    gu = _dot(hb_scr[...], wgu_ref[...])
    gate, up = gu[:, :f], gu[:, f:]
    comb = comb_ref[...]
    lane = lax.broadcasted_iota(jnp.int32, comb.shape, 1)
    c = jnp.sum(jnp.where(lane == e, comb, 0.0), axis=1, keepdims=True)
    act = gate / (1.0 + jnp.exp(-gate)) * up * c
    acc_scr[...] += _dot(act.astype(BF16), wd_ref[...])

    @pl.when(e == pl.num_programs(1) - 1)
    def _():
        o_ref[...] = _layer_norm(alpha * h_ref[...] + acc_scr[...], lng_ref[...], lnb_ref[...])


def _moe(h, comb, wgu, wd, ln_g, ln_b, *, alpha, name):
    t, d = h.shape
    tm = next(c for c in (1024, 512, 256) if t % c == 0)
    n_e, _, f2 = wgu.shape
    row = lambda i, e: (i, 0)
    const = lambda i, e: (0, 0)
    return pl.pallas_call(
        functools.partial(_moe_kernel, alpha=alpha),
        grid=(t // tm, n_e),
        in_specs=[pl.BlockSpec((tm, d), row), pl.BlockSpec((tm, LANES), row),
                  pl.BlockSpec((None, d, f2), lambda i, e: (e, 0, 0)),
                  pl.BlockSpec((None, f2 // 2, d), lambda i, e: (e, 0, 0)),
                  pl.BlockSpec((1, d), const), pl.BlockSpec((1, d), const)],
        out_specs=pl.BlockSpec((tm, d), row),
        out_shape=jax.ShapeDtypeStruct((t, d), F32),
        scratch_shapes=[pltpu.VMEM((tm, d), BF16), pltpu.VMEM((tm, d), F32)],
        compiler_params=pltpu.CompilerParams(dimension_semantics=("arbitrary", "arbitrary"),
                                             vmem_limit_bytes=VMEM_LIMIT),
        name=name,
    )(h, comb, wgu, wd, ln_g, ln_b)


def kernel(x_prompt, x_sample, cache_moba_k, cache_moba_v, cache_diff_k, cache_diff_v, page_table, w_in, b_gate, lambda_q1, lambda_k1, lambda_q2, lambda_k2, subln_w, w_proj_a, w_proj_b, w_out, ln1_g, ln1_b, ln2_g, ln2_b, w_route_group, b_route_group, w_route_expert, b_route_expert, w_expert_gate, w_expert_up, w_expert_down):
    depth = w_in.shape[0]
    bsz, seq, d = x_prompt.shape
    n_seq, n_new, _ = x_sample.shape
    alpha = (2 * depth) ** 0.25
    slopes_a = _alibi_slopes(A_HEADS) * LOG2E
    slopes_b = _alibi_slopes(B_HEADS) * LOG2E
    slopes_a_smem = jnp.asarray(slopes_a, F32)
    slopes_b_smem = jnp.asarray(slopes_b, F32)
    h_p = x_prompt.reshape(bsz * seq, d)
    h_s = x_sample.reshape(n_seq * n_new, d)
    rows_p, rows_s = [], []
    for l in range(depth):
        lam_init = _lambda_init(l)
        lam = (jnp.exp(jnp.sum(lambda_q1[l].astype(F32) * lambda_k1[l].astype(F32)))
               - jnp.exp(jnp.sum(lambda_q2[l].astype(F32) * lambda_k2[l].astype(F32))) + lam_init).reshape(1)
        w_bf = w_in[l].astype(BF16)
        w_t = [w_bf[:, slot * WIDTH:(slot + 1) * WIDTH].T for slot in (1, 2, 5)]
        subw = subln_w[l].reshape(1, B_V_DIM)
        wpa, wpb, wo = w_proj_a[l].astype(BF16), w_proj_b[l].astype(BF16), w_out[l].astype(BF16)
        n_route = N_EXPERTS + N_GROUPS
        w_router = jnp.pad(jnp.concatenate([w_route_expert[l], w_route_group[l]], axis=1),
                           ((0, 0), (0, LANES - n_route))).astype(BF16)
        b_router = jnp.pad(jnp.concatenate([b_route_expert[l], b_route_group[l]]), (0, LANES - n_route)).reshape(1, LANES)
        wgu = jnp.concatenate([w_expert_gate[l], w_expert_up[l]], axis=2).astype(BF16)
        wd = w_expert_down[l].astype(BF16)
        ln1 = (ln1_g[l].reshape(1, d), ln1_b[l].reshape(1, d))
        ln2 = (ln2_g[l].reshape(1, d), ln2_b[l].reshape(1, d))

        (qa, qb, kb, vb, gates, ka_bf, kb_bf, kat, vat, vat_bf, vbt_bf, kmean) = _in_projection(
            h_p, w_bf, b_gate[l], w_t, prompt=True, bsz=bsz)
        o_a = _moba_prompt(qa, ka_bf, vat_bf, kmean.reshape(bsz, seq // MOBA_BLOCK, WIDTH), slopes_a_smem,
                           bsz=bsz, seq=seq)
        o_b = _diff_prompt(qb, kb_bf, vbt_bf, slopes_b_smem, lam, subw, bsz=bsz, seq=seq, lam_init=lam_init)
        h1, comb = _merge(h_p, o_a, o_b, gates, wpa, wpb, wo, *ln1, w_router, b_router, alpha=alpha,
                          name="merge_prompt")
        h_p = _moe(h1, comb, wgu, wd, *ln2, alpha=alpha, name="moe_prompt")
        from_feat = lambda xt: jnp.transpose(xt.reshape(bsz, A_HEADS, A_HEAD_DIM, seq), (0, 3, 1, 2))
        rows_p.append((from_feat(kat), from_feat(vat),
                       kb.reshape(bsz, seq, B_HEADS, 2 * B_HEAD_DIM), vb.reshape(bsz, seq, B_HEADS, B_V_DIM)))

        qa, ka, va, qb, kb, vb, gates = _in_projection(h_s, w_bf, b_gate[l], None, prompt=False)
        o_a, o_b = _sample_attention(qa, ka, va, qb, kb, vb, cache_moba_k[l], cache_moba_v[l], cache_diff_k[l],
                                     cache_diff_v[l], page_table, slopes_a, slopes_b, lam, subw, lam_init=lam_init)
        h1, comb = _merge(h_s, o_a, o_b, gates, wpa, wpb, wo, *ln1, w_router, b_router, alpha=alpha,
                          name="merge_sample")
        h_s = _moe(h1, comb, wgu, wd, *ln2, alpha=alpha, name="moe_sample")
        rows_s.append((ka.reshape(n_seq, n_new, A_HEADS, A_HEAD_DIM), va.reshape(n_seq, n_new, A_HEADS, A_HEAD_DIM),
                       kb.reshape(n_seq, n_new, B_HEADS, 2 * B_HEAD_DIM), vb.reshape(n_seq, n_new, B_HEADS, B_V_DIM)))

    stack = lambda rows, k: jnp.stack([r[k] for r in rows])
    return (h_p.reshape(bsz, seq, d), h_s.reshape(n_seq, n_new, d),
            stack(rows_p, 0), stack(rows_p, 1), stack(rows_p, 2), stack(rows_p, 3),
            stack(rows_s, 0), stack(rows_s, 1), stack(rows_s, 2), stack(rows_s, 3))
```

```python
import functools
import math

import numpy as np
import jax
import jax.numpy as jnp
from jax import lax
from jax.experimental import pallas as pl
from jax.experimental.pallas import tpu as pltpu

F32 = jnp.float32
BF16 = jnp.bfloat16

A_HEADS = 8
A_HEAD_DIM = 64
B_HEADS = 4
B_HEAD_DIM = 64
B_V_DIM = 2 * B_HEAD_DIM
WIDTH = A_HEADS * A_HEAD_DIM
MOBA_BLOCK = 256
MOBA_TOPK = 3
N_GROUPS = 4
EXPERTS_PER_GROUP = 4
N_EXPERTS = N_GROUPS * EXPERTS_PER_GROUP
LN_EPS = 1e-5
LANES = 128
HEAD_LANES = 64
GROUP_LANES = 256
CHAINS = GROUP_LANES // HEAD_LANES
NEG_INF = float("-inf")
POS_INF = float("inf")
LOG2E = math.log2(math.e)
VMEM_LIMIT = 56 * 1024 * 1024
SAMPLE_PAGES_PER_STEP = 16

_NT = (((1,), (1,)), ((), ()))


def _lambda_init(layer):
    return 0.8 - 0.6 * math.exp(-0.3 * layer)


def _alibi_slopes(n_heads):
    return np.power(2.0, -8.0 * np.arange(1, n_heads + 1) / n_heads)


def _smem_spec():
    return pl.BlockSpec(memory_space=pltpu.SMEM)


def _layer_norm(x, g, b):
    mu = jnp.mean(x, axis=-1, keepdims=True)
    xc = x - mu
    var = jnp.mean(xc * xc, axis=-1, keepdims=True)
    return xc * lax.rsqrt(var + LN_EPS) * g + b


def _dot(a, b):
    return jnp.dot(a, b, preferred_element_type=F32)


def _dot_nt(a, b):
    return lax.dot_general(a, b, _NT, preferred_element_type=F32)


def _inproj_kernel(x_ref, w_ref, bg_ref, *refs, prompt):
    if prompt:
        (wkat_ref, wvat_ref, wvbt_ref, qa_ref, qb_ref, kb_ref, vb_ref, g_ref,
         kab_ref, kbb_ref, kat_ref, vat_ref, vatb_ref, vbtb_ref, kmean_ref) = refs
    else:
        qa_ref, ka_ref, va_ref, qb_ref, kb_ref, vb_ref, g_ref = refs
    xb = x_ref[...].astype(BF16)

    def proj(slot):
        return _dot(xb, w_ref[:, slot * WIDTH:(slot + 1) * WIDTH])

    q_scale = A_HEAD_DIM ** -0.5 * LOG2E
    qa_ref[...] = (proj(0) * q_scale).astype(qa_ref.dtype)
    qb_ref[...] = (proj(3) * q_scale).astype(qb_ref.dtype)
    ka = proj(1)
    kb = proj(4)
    vb = proj(5)
    z = _dot(xb, w_ref[:, 6 * WIDTH:]) + bg_ref[...]
    g_ref[...] = (1.0 / (1.0 + jnp.exp(-z))).astype(g_ref.dtype)
    if prompt:
        for h in range(B_HEADS):
            kb_ref[:, h, :] = kb[:, h * B_V_DIM:(h + 1) * B_V_DIM]
            vb_ref[:, h, :] = vb[:, h * B_V_DIM:(h + 1) * B_V_DIM]
        kab_ref[...] = ka.astype(BF16)
        kbb_ref[...] = kb.astype(BF16)
        kmean_ref[...] = jnp.mean(ka, axis=0, keepdims=True)
        kat_ref[...] = _dot_nt(wkat_ref[...], xb)
        vat = _dot_nt(wvat_ref[...], xb)
        vat_ref[...] = vat
        vatb_ref[...] = vat.astype(BF16)
        vbtb_ref[...] = _dot_nt(wvbt_ref[...], xb).astype(BF16)
    else:
        ka_ref[...] = ka
        va_ref[...] = proj(2)
        kb_ref[...] = kb
        vb_ref[...] = vb


def _in_projection(x, w_bf, b_gate, w_t, *, prompt, bsz=1):
    t, d = x.shape
    tm = MOBA_BLOCK
    assert t % (bsz * tm) == 0
    seq = t // bsz
    nt = seq // tm
    n_in = w_bf.shape[1]
    row = lambda i: (i, 0)
    const = lambda i: (0, 0)
    rows_f32 = jax.ShapeDtypeStruct((t, WIDTH), F32)
    rows_bf = jax.ShapeDtypeStruct((t, WIDTH), BF16)
    row_spec = pl.BlockSpec((tm, WIDTH), row)
    in_specs = [pl.BlockSpec((tm, d), row), pl.BlockSpec((d, n_in), const), pl.BlockSpec((1, 2 * d), const)]
    args = [x, w_bf, b_gate.reshape(1, 2 * d)]
    gate_shape = jax.ShapeDtypeStruct((t, 2 * d), BF16)
    gate_spec = pl.BlockSpec((tm, 2 * d), row)
    if prompt:
        in_specs += [pl.BlockSpec((WIDTH, d), const)] * 3
        args += list(w_t)
        feat_f32 = jax.ShapeDtypeStruct((bsz, WIDTH, seq), F32)
        feat_bf = jax.ShapeDtypeStruct((bsz, WIDTH, seq), BF16)
        feat_spec = pl.BlockSpec((None, WIDTH, tm), lambda i: (i // nt, 0, i % nt))
        heads_f32 = jax.ShapeDtypeStruct((t, B_HEADS, B_V_DIM), F32)
        heads_spec = pl.BlockSpec((tm, B_HEADS, B_V_DIM), lambda i: (i, 0, 0))
        out_shape = [rows_bf, rows_bf, heads_f32, heads_f32, gate_shape, rows_bf, rows_bf,
                     feat_f32, feat_f32, feat_bf, feat_bf, jax.ShapeDtypeStruct((t // tm, 1, WIDTH), F32)]
        out_specs = [row_spec] * 2 + [heads_spec] * 2 + [gate_spec] + [row_spec] * 2 + [feat_spec] * 4
        out_specs += [pl.BlockSpec((None, 1, WIDTH), lambda i: (i, 0, 0))]
    else:
        out_shape = [rows_f32] * 6 + [gate_shape]
        out_specs = [row_spec] * 6 + [gate_spec]
    return pl.pallas_call(
        functools.partial(_inproj_kernel, prompt=prompt),
        grid=(t // tm,),
        in_specs=in_specs, out_specs=out_specs, out_shape=out_shape,
        compiler_params=pltpu.CompilerParams(dimension_semantics=("arbitrary",), vmem_limit_bytes=VMEM_LIMIT),
        name="in_projection_prompt" if prompt else "in_projection_sample",
    )(*args)


def _attend_tile(i, k_ref, vt_ref, qm_scr, bias_scr, m_scr, l_scr, acc_scr, u_scr, p_scr, slopes, vt_rows, sel_row):
    tq = qm_scr.shape[1]
    key_row = lax.broadcasted_iota(jnp.int32, (MOBA_BLOCK, tq), 0)
    causal = key_row <= lax.broadcasted_iota(jnp.int32, (MOBA_BLOCK, tq), 1)
    key_off = key_row.astype(F32)
    for c in range(CHAINS):
        bias_scr[c] = slopes[c] * key_off
        m_scr[c] = jnp.full((1, tq), NEG_INF, F32)
        l_scr[c] = jnp.zeros((1, tq), F32)
        acc_scr[c] = jnp.zeros(acc_scr.shape[1:], F32)

    def scores(j, slot):
        kj = k_ref[pl.ds(pl.multiple_of(j * MOBA_BLOCK, MOBA_BLOCK), MOBA_BLOCK), :]
        for c in range(CHAINS):
            u_scr[slot * CHAINS + c] = _dot_nt(kj, qm_scr[c])

    def block(j, slot, own):
        start = pl.multiple_of(j * MOBA_BLOCK, MOBA_BLOCK)
        alphas = []
        for c in range(CHAINS):
            u = u_scr[slot * CHAINS + c] + bias_scr[c]
            m_old = m_scr[c]
            if own:
                u = jnp.where(causal, u, NEG_INF)
                m_new = jnp.maximum(m_old, jnp.max(u, axis=0, keepdims=True))
                shift = m_new
            else:
                cj = slopes[c] * ((j - i) * MOBA_BLOCK).astype(F32)
                m_blk = jnp.max(u, axis=0, keepdims=True) + cj
                pick = sel_row(c, j)
                if pick is not None:
                    m_blk = jnp.where(pick > 0.0, m_blk, NEG_INF)
                m_new = jnp.maximum(m_old, m_blk)
                shift = m_new - cj
                if pick is not None:
                    shift = jnp.where(pick > 0.0, shift, POS_INF)
            p = jnp.exp2(u - shift)
            alpha = jnp.exp2(m_old - m_new)
            m_scr[c] = m_new
            l_scr[c] = alpha * l_scr[c] + jnp.sum(p, axis=0, keepdims=True)
            p_scr[c] = p.astype(BF16)
            alphas.append(alpha)
        for c in range(CHAINS):
            vtj = vt_ref[vt_rows[c], pl.ds(start, MOBA_BLOCK)]
            acc_scr[c] = alphas[c] * acc_scr[c] + _dot(vtj, p_scr[c])

    scores(i, 0)
    scores(0, 1)
    block(i, 0, True)

    def pair(jj, carry):
        j = 2 * jj
        scores(j + 1, 0)
        block(j, 1, False)
        scores(jnp.minimum(j + 2, i), 1)
        block(j + 1, 0, False)
        return carry

    lax.fori_loop(0, i // 2, pair, 0)

    @pl.when(i % 2 == 1)
    def _():
        block(i - 1, 1, False)


def _moba_prompt_kernel(slopes_ref, q_ref, k_ref, vt_ref, kmean_ref, o_ref,
                        qm_scr, sel_scr, bias_scr, m_scr, l_scr, acc_scr, u_scr, p_scr, *, nb):
    g = pl.program_id(1)
    i = pl.program_id(2)
    tq = q_ref.shape[0]
    q = q_ref[...]
    lane_head = lax.broadcasted_iota(jnp.int32, (1, GROUP_LANES), 1) // HEAD_LANES
    km = kmean_ref[...].astype(BF16)
    blk = lax.broadcasted_iota(jnp.int32, (nb, tq), 0)
    past = blk < i
    for h in range(CHAINS):
        qm = jnp.where(lane_head == h, q, jnp.zeros_like(q))
        qm_scr[h] = qm
        gate = jnp.where(past, _dot_nt(km, qm), NEG_INF)
        cnt = jnp.zeros((nb, tq), jnp.int32)
        for jp in range(nb):
            gj = gate[jp:jp + 1, :]
            beats = (gj > gate) | ((gj == gate) & (blk > jp))
            cnt = cnt + beats.astype(jnp.int32)
        sel_scr[h] = jnp.where((cnt < MOBA_TOPK) & past, 1.0, 0.0).astype(F32)

    slopes = [slopes_ref[g * CHAINS + h] for h in range(CHAINS)]
    vt_rows = [pl.ds((h // 2) * LANES, LANES) for h in range(CHAINS)]
    _attend_tile(i, k_ref, vt_ref, qm_scr, bias_scr, m_scr, l_scr, acc_scr, u_scr, p_scr, slopes, vt_rows,
                 lambda c, j: sel_scr[c, pl.ds(j, 1), :])
    outs = []
    for h in range(CHAINS):
        half = (h % 2) * HEAD_LANES
        outs.append(acc_scr[h, half:half + HEAD_LANES, :] * (1.0 / l_scr[h]))
    o_ref[...] = jnp.concatenate(outs, axis=0).T.astype(o_ref.dtype)


def _attn_scratch(tq, acc_rows):
    return [pltpu.VMEM((CHAINS, MOBA_BLOCK, tq), F32), pltpu.VMEM((CHAINS, 1, tq), F32),
            pltpu.VMEM((CHAINS, 1, tq), F32), pltpu.VMEM((CHAINS, acc_rows, tq), F32),
            pltpu.VMEM((2 * CHAINS, MOBA_BLOCK, tq), F32), pltpu.VMEM((CHAINS, MOBA_BLOCK, tq), BF16)]


def _attn_specs(bsz, seq):
    nq = seq // MOBA_BLOCK
    q_spec = pl.BlockSpec((MOBA_BLOCK, GROUP_LANES), lambda b, g, i: (b * nq + i, g))
    k_spec = pl.BlockSpec((seq, GROUP_LANES), lambda b, g, i: (b, g))
    vt_spec = pl.BlockSpec((None, GROUP_LANES, seq), lambda b, g, i: (b, g, 0))
    return nq, q_spec, k_spec, vt_spec


def _moba_prompt(q_bf, k_bf, vt_bf, kmean, slopes, *, bsz, seq):
    nq, q_spec, k_spec, vt_spec = _attn_specs(bsz, seq)
    return pl.pallas_call(
        functools.partial(_moba_prompt_kernel, nb=nq),
        grid=(bsz, WIDTH // GROUP_LANES, nq),
        in_specs=[_smem_spec(), q_spec, k_spec, vt_spec,
                  pl.BlockSpec((None, nq, GROUP_LANES), lambda b, g, i: (b, 0, g))],
        out_specs=q_spec,
        out_shape=jax.ShapeDtypeStruct((bsz * seq, WIDTH), BF16),
        scratch_shapes=[pltpu.VMEM((CHAINS, MOBA_BLOCK, GROUP_LANES), BF16),
                        pltpu.VMEM((CHAINS, nq, MOBA_BLOCK), F32)] + _attn_scratch(MOBA_BLOCK, LANES),
        compiler_params=pltpu.CompilerParams(dimension_semantics=("arbitrary",) * 3, vmem_limit_bytes=VMEM_LIMIT),
        name="moba_prompt",
    )(slopes, q_bf, k_bf, vt_bf, kmean)


def _diff_finish(o1, o2, lam, subw, lam_init):
    o = o1 - lam * o2
    o = o * lax.rsqrt(jnp.mean(o * o, axis=-1, keepdims=True) + LN_EPS)
    return o * subw * (1.0 - lam_init)


def _diff_prompt_kernel(slopes_ref, lam_ref, q_ref, k_ref, vt_ref, subw_ref, o_ref,
                        qm_scr, bias_scr, m_scr, l_scr, acc_scr, u_scr, p_scr, *, lam_init):
    g = pl.program_id(1)
    i = pl.program_id(2)
    heads = CHAINS // 2
    q = q_ref[...]
    lane_map = lax.broadcasted_iota(jnp.int32, (1, GROUP_LANES), 1) // HEAD_LANES
    for c in range(CHAINS):
        qm_scr[c] = jnp.where(lane_map == c, q, jnp.zeros_like(q))
    slopes = [slopes_ref[g * heads + c // 2] for c in range(CHAINS)]
    vt_rows = [pl.ds((c // 2) * B_V_DIM, B_V_DIM) for c in range(CHAINS)]
    _attend_tile(i, k_ref, vt_ref, qm_scr, bias_scr, m_scr, l_scr, acc_scr, u_scr, p_scr, slopes, vt_rows, lambda c, j: None)
    lam = lam_ref[0]
    for h in range(heads):
        o1, o2 = [(acc_scr[2 * h + mm] * (1.0 / l_scr[2 * h + mm])).T for mm in range(2)]
        o = _diff_finish(o1, o2, lam, subw_ref[...], lam_init)
        o_ref[:, h * B_V_DIM:(h + 1) * B_V_DIM] = o.astype(o_ref.dtype)


def _diff_prompt(q_bf, k_bf, vt_bf, slopes, lam, subw, *, bsz, seq, lam_init):
    nq, q_spec, k_spec, vt_spec = _attn_specs(bsz, seq)
    return pl.pallas_call(
        functools.partial(_diff_prompt_kernel, lam_init=lam_init),
        grid=(bsz, WIDTH // GROUP_LANES, nq),
        in_specs=[_smem_spec(), _smem_spec(), q_spec, k_spec, vt_spec,
                  pl.BlockSpec((1, B_V_DIM), lambda b, g, i: (0, 0))],
        out_specs=q_spec,
        out_shape=jax.ShapeDtypeStruct((bsz * seq, WIDTH), BF16),
        scratch_shapes=[pltpu.VMEM((CHAINS, MOBA_BLOCK, GROUP_LANES), BF16)] + _attn_scratch(MOBA_BLOCK, B_V_DIM),
        compiler_params=pltpu.CompilerParams(dimension_semantics=("arbitrary",) * 3, vmem_limit_bytes=VMEM_LIMIT),
        name="diff_prompt",
    )(slopes, lam, q_bf, k_bf, vt_bf, subw)


def _head_rows(x, n_new):
    chunks = WIDTH // HEAD_LANES
    tiled = jnp.concatenate([x] * chunks, axis=0)
    r = lax.broadcasted_iota(jnp.int32, (chunks * n_new, WIDTH), 0) // n_new
    c = lax.broadcasted_iota(jnp.int32, (chunks * n_new, WIDTH), 1) // HEAD_LANES
    return jnp.where(r == c, tiled, 0.0).astype(BF16)


def _map_rows(x, n_new):
    lane_map = lax.broadcasted_iota(jnp.int32, (n_new, B_V_DIM), 1) // HEAD_LANES
    pieces = []
    for h in range(B_HEADS):
        xh = x[:, h * B_V_DIM:(h + 1) * B_V_DIM]
        for mm in range(2):
            pieces.append(jnp.where(lane_map == mm, xh, 0.0))
    return jnp.concatenate(pieces, axis=0).astype(BF16)


def _head_diag(full, n_new):
    lane_head = lax.broadcasted_iota(jnp.int32, (n_new, WIDTH), 1) // HEAD_LANES
    out = jnp.zeros((n_new, WIDTH), F32)
    for h in range(WIDTH // HEAD_LANES):
        out = out + jnp.where(lane_head == h, full[h * n_new:(h + 1) * n_new, :], 0.0)
    return out


def _col(x):
    return x[:, :1]


def _rep(x, rows):
    return jnp.broadcast_to(x, (rows, LANES))


def _sample_attn_kernel(pt_ref, lam_ref, qa_ref, qb_ref, kan_ref, van_ref, kbn_ref, vbn_ref, subw_ref,
                        sla_ref, slb_ref, bias_a_ref, bias_b_ref, *refs, nblk, n_new, lam_init):
    del pt_ref
    pps = SAMPLE_PAGES_PER_STEP
    ak, av, bk, bv = (refs[n * pps:(n + 1) * pps] for n in range(4))
    (oa_ref, ob_ref, qa_scr, qb_scr, qd_scr, gate_scr, mst_scr, lst_scr, acc_scr,
     dm_scr, dl_scr, dacc_scr, sd_scr, pd_scr) = refs[4 * pps:]
    step = pl.program_id(1)
    rows = qa_scr.shape[0]
    past_len = nblk * MOBA_BLOCK
    blocks_per_step = pps // 2
    lane = lax.broadcasted_iota(jnp.int32, (rows, LANES), 1)

    @pl.when(step == 0)
    def _():
        qa_scr[...] = _head_rows(qa_ref[...], n_new)
        qb_scr[...] = _head_rows(qb_ref[...], n_new)
        qd_scr[...] = _map_rows(qb_ref[...], n_new)
        gate_scr[...] = jnp.full(gate_scr.shape, NEG_INF, F32)
        mst_scr[...] = jnp.full(mst_scr.shape, NEG_INF, F32)
        lst_scr[...] = jnp.zeros(lst_scr.shape, F32)
        dm_scr[...] = jnp.full(dm_scr.shape, NEG_INF, F32)
        dl_scr[...] = jnp.zeros(dl_scr.shape, F32)
        dacc_scr[...] = jnp.zeros(dacc_scr.shape, F32)

    sla = sla_ref[...]
    slb = slb_ref[...]
    qa_rows = qa_scr[...]
    qd_rows = qd_scr[...]
    dcols = bk[0].shape[0]

    sa = [_dot(qa_rows, ak[t][...].astype(BF16)) for t in range(pps)]
    for t in range(pps):
        sd_scr[:, t * dcols:(t + 1) * dcols] = _dot_nt(qd_rows, bk[t][...].astype(BF16))

    pa = []
    for b in range(blocks_per_step):
        jb = step * blocks_per_step + b
        blk_off = (jb * MOBA_BLOCK - past_len).astype(F32)
        here = lane == jb
        s_raw = jnp.concatenate(sa[2 * b:2 * b + 2], axis=1)
        gate = jnp.sum(s_raw, axis=1, keepdims=True) * (1.0 / MOBA_BLOCK)
        s = s_raw + bias_a_ref[...]
        m_loc = jnp.max(s, axis=1, keepdims=True)
        p = jnp.exp2(s - m_loc)
        l_loc = jnp.sum(p, axis=1, keepdims=True)
        pa.append(p.astype(BF16))
        gate_scr[...] = jnp.where(here, _rep(gate, rows), gate_scr[...])
        mst_scr[...] = jnp.where(here, _rep(m_loc, rows) + sla * blk_off, mst_scr[...])
        lst_scr[...] = jnp.where(here, _rep(l_loc, rows), lst_scr[...])

    step_off = _col(slb) * (step * (pps * LANES) - past_len).astype(F32)
    m_old = _col(dm_scr[...])
    m_loc = jnp.full((rows, 1), NEG_INF, F32)
    for t in range(pps):
        cols = slice(t * dcols, (t + 1) * dcols)
        m_loc = jnp.maximum(m_loc, jnp.max(sd_scr[:, cols] + bias_b_ref[:, cols], axis=1, keepdims=True))
    m_new = jnp.maximum(m_old, m_loc + step_off)
    shift = m_new - step_off
    l_loc = jnp.zeros((rows, 1), F32)
    for t in range(pps):
        cols = slice(t * dcols, (t + 1) * dcols)
        p = jnp.exp2(sd_scr[:, cols] + bias_b_ref[:, cols] - shift)
        l_loc = l_loc + jnp.sum(p, axis=1, keepdims=True)
        pd_scr[:, cols] = p.astype(BF16)
    alpha = jnp.exp2(m_old - m_new)
    dl_scr[...] = _rep(alpha * _col(dl_scr[...]) + l_loc, rows)
    dm_scr[...] = _rep(m_new, rows)

    for b in range(blocks_per_step):
        acc_scr[step * blocks_per_step + b] = sum(
            _dot_nt(pa[b][:, t * LANES:(t + 1) * LANES], av[2 * b + t][...].astype(BF16)) for t in range(2))
    dacc_scr[...] = alpha * dacc_scr[...] + sum(
        _dot(pd_scr[:, t * dcols:(t + 1) * dcols], bv[t][...].astype(BF16)) for t in range(pps))

    @pl.when(step == pl.num_programs(1) - 1)
    def _():
        pad = jnp.zeros((LANES - n_new, WIDTH), F32)
        qry = lax.broadcasted_iota(jnp.int32, (rows, LANES), 0) % n_new
        causal = lane <= qry
        new_off = lane.astype(F32)

        def new_block(q_rows, k_new, v_new, slope):
            kn = jnp.concatenate([k_new, pad], axis=0).astype(BF16)
            vn = jnp.concatenate([v_new, pad], axis=0).astype(BF16)
            sn = _dot_nt(q_rows, kn) + slope * new_off
            return jnp.where(causal, sn, NEG_INF), vn

        sn, vn = new_block(qa_rows, kan_ref[...], van_ref[...], sla)
        m_own = jnp.max(sn, axis=1, keepdims=True)
        pn = jnp.exp2(sn - m_own)
        l_own = jnp.sum(pn, axis=1, keepdims=True)
        acc_own = _dot(pn.astype(BF16), vn)
        gate = gate_scr[...]
        cnt = jnp.zeros((rows, LANES), jnp.int32)
        for jp in range(nblk):
            gj = gate[:, jp:jp + 1]
            beats = (gj > gate) | ((gj == gate) & (lane > jp))
            cnt = cnt + beats.astype(jnp.int32)
        sel = (cnt < MOBA_TOPK) & (lane < nblk)
        mst = mst_scr[...]
        m_all = jnp.maximum(m_own, jnp.max(jnp.where(sel, mst, NEG_INF), axis=1, keepdims=True))
        w = jnp.where(sel, jnp.exp2(mst - m_all), 0.0)
        w_own = jnp.exp2(m_own - m_all)
        l_all = w_own * l_own + jnp.sum(w * lst_scr[...], axis=1, keepdims=True)
        acc = w_own * acc_own
        for jb in range(nblk):
            acc = acc + w[:, jb:jb + 1] * acc_scr[jb]
        oa_ref[...] = _head_diag(acc * (1.0 / l_all), n_new)

        sn, vn = new_block(qb_scr[...], kbn_ref[...], vbn_ref[...], slb)
        m_old = _col(dm_scr[...])
        m_new = jnp.maximum(m_old, jnp.max(sn, axis=1, keepdims=True))
        pn = jnp.exp2(sn - m_new)
        alpha = jnp.exp2(m_old - m_new)
        l_d = alpha * _col(dl_scr[...]) + jnp.sum(pn, axis=1, keepdims=True)
        new_full = _dot(pn.astype(BF16), vn)
        per_head = rows // B_HEADS
        new_acc = jnp.concatenate([new_full[h * per_head:(h + 1) * per_head, h * B_V_DIM:(h + 1) * B_V_DIM]
                                   for h in range(B_HEADS)], axis=0)
        full = (alpha * dacc_scr[...] + new_acc) * (1.0 / l_d)
        lam = lam_ref[0]
        for h in range(B_HEADS):
            o1 = full[h * per_head:h * per_head + n_new, :]
            o2 = full[h * per_head + n_new:(h + 1) * per_head, :]
            ob_ref[:, h * B_V_DIM:(h + 1) * B_V_DIM] = _diff_finish(o1, o2, lam, subw_ref[...], lam_init)


def _sample_attention(qa, ka, va, qb, kb, vb, pool_ak, pool_av, pool_bk, pool_bv, page_table,
                      slopes_a, slopes_b, lam, subw, *, lam_init):
    n_seq, n_pages = page_table.shape
    n_new = qa.shape[0] // n_seq
    n_pool, page = pool_ak.shape[:2]
    pps = SAMPLE_PAGES_PER_STEP
    assert page == LANES and 2 * page == MOBA_BLOCK and n_pages % pps == 0 and n_new <= LANES
    nblk = n_pages // 2
    assert nblk <= LANES
    rows = (WIDTH // HEAD_LANES) * n_new
    moba_pages = [jnp.transpose(p, (0, 2, 3, 1)).reshape(n_pool, WIDTH, page) for p in (pool_ak, pool_av)]
    diff_pages = [p.reshape(n_pool, page * B_HEADS, B_V_DIM) for p in (pool_bk, pool_bv)]
    per_seq = [x.reshape(n_seq, n_new, WIDTH) for x in (qa, qb, ka, va, kb, vb)]

    row_slope_a = np.repeat(slopes_a, n_new)[:, None]
    row_slope_b = np.repeat(slopes_b, 2 * n_new)[:, None]
    row_head_b = np.repeat(np.arange(B_HEADS), 2 * n_new)[:, None]
    key_a = np.arange(MOBA_BLOCK)[None, :]
    col = np.arange(pps * page * B_HEADS)[None, :]
    bias_a = row_slope_a * key_a
    bias_b = np.where(col % B_HEADS == row_head_b, row_slope_b * (col // B_HEADS), NEG_INF)
    consts = [jnp.asarray(np.broadcast_to(row_slope_a, (rows, LANES)), F32),
              jnp.asarray(np.broadcast_to(row_slope_b, (rows, LANES)), F32),
              jnp.asarray(bias_a, F32), jnp.asarray(bias_b, F32)]

    seq_spec = pl.BlockSpec((None, n_new, WIDTH), lambda s, j, pt: (s, 0, 0))
    const2 = lambda s, j, pt: (0, 0)

    def page_spec(which, shape):
        return pl.BlockSpec((None,) + shape, lambda s, j, pt: (pt[s * n_pages + pps * j + which], 0, 0))

    cache_specs, cache_args = [], []
    for pool in moba_pages + diff_pages:
        for which in range(pps):
            cache_specs.append(page_spec(which, pool.shape[1:]))
            cache_args.append(pool)
    stat = pltpu.VMEM((rows, LANES), F32)
    grid_spec = pltpu.PrefetchScalarGridSpec(
        num_scalar_prefetch=1,
        grid=(n_seq, n_pages // pps),
        in_specs=[_smem_spec()] + [seq_spec] * 6 + [pl.BlockSpec((1, B_V_DIM), const2)]
                 + [pl.BlockSpec(c.shape, const2) for c in consts] + cache_specs,
        out_specs=[seq_spec, seq_spec],
        scratch_shapes=[pltpu.VMEM((rows, WIDTH), BF16), pltpu.VMEM((rows, WIDTH), BF16),
                        pltpu.VMEM((rows, B_V_DIM), BF16), stat, stat, stat,
                        pltpu.VMEM((nblk, rows, WIDTH), F32), stat, stat, pltpu.VMEM((rows, B_V_DIM), F32),
                        pltpu.VMEM((rows, pps * page * B_HEADS), F32),
                        pltpu.VMEM((rows, pps * page * B_HEADS), BF16)])
    o_a, o_b = pl.pallas_call(
        functools.partial(_sample_attn_kernel, nblk=nblk, n_new=n_new, lam_init=lam_init),
        grid_spec=grid_spec,
        out_shape=[jax.ShapeDtypeStruct((n_seq, n_new, WIDTH), F32)] * 2,
        compiler_params=pltpu.CompilerParams(dimension_semantics=("arbitrary", "arbitrary"),
                                             vmem_limit_bytes=VMEM_LIMIT),
        name="sample_attention",
    )(page_table.reshape(-1), lam, *per_seq, subw, *consts, *cache_args)
    return o_a.reshape(n_seq * n_new, WIDTH), o_b.reshape(n_seq * n_new, WIDTH)


def _route(logits):
    lane_i = lax.broadcasted_iota(jnp.int32, logits.shape, 1)
    lane = lane_i.astype(F32)
    lane_grp = (lane_i // EXPERTS_PER_GROUP).astype(F32)

    def top(vals):
        v = jnp.max(vals, axis=1, keepdims=True)
        idx = jnp.min(jnp.where(vals == v, lane, float(LANES)), axis=1, keepdims=True)
        return v, idx

    is_grp = (lane_i >= N_EXPERTS) & (lane_i < N_EXPERTS + N_GROUPS)
    lg = jnp.where(is_grp, logits, NEG_INF)
    g_max, g_idx = top(lg)
    p_grp = 1.0 / jnp.sum(jnp.exp(lg - g_max), axis=1, keepdims=True)
    g_sel = g_idx - float(N_EXPERTS)
    in_grp = (lane_i < N_EXPERTS) & (lane_grp == g_sel)
    le = jnp.where(in_grp, logits, NEG_INF)
    v1, i1 = top(le)
    v2, i2 = top(jnp.where(lane == i1, NEG_INF, le))
    e2 = jnp.exp(v2 - v1)
    w1 = 1.0 / (1.0 + e2)
    w2 = e2 / (1.0 + e2)
    return jnp.where(lane == i1, w1 * p_grp, 0.0) + jnp.where(lane == i2, w2 * p_grp, 0.0)


def _merge_kernel(x_ref, oa_ref, ob_ref, g_ref, wpa_ref, wpb_ref, wo_ref, lng_ref, lnb_ref, wr_ref, br_ref,
                  h_ref, comb_ref, *, alpha):
    d = x_ref.shape[1]
    ya = _dot(oa_ref[...].astype(BF16), wpa_ref[...])
    yb = _dot(ob_ref[...].astype(BF16), wpb_ref[...])
    y = g_ref[:, :d] * ya + g_ref[:, d:] * yb
    mix = _dot(y.astype(BF16), wo_ref[...])
    h = _layer_norm(alpha * x_ref[...] + mix, lng_ref[...], lnb_ref[...])
    h_ref[...] = h
    logits = _dot(h.astype(BF16), wr_ref[...]) + br_ref[...]
    comb_ref[...] = _route(logits)


def _merge(x, o_a, o_b, gates, wpa, wpb, wo, ln_g, ln_b, w_router, b_router, *, alpha, name):
    t, d = x.shape
    tm = 256
    assert t % tm == 0
    row = lambda i: (i, 0)
    const = lambda i: (0, 0)
    return pl.pallas_call(
        functools.partial(_merge_kernel, alpha=alpha),
        grid=(t // tm,),
        in_specs=[pl.BlockSpec((tm, d), row), pl.BlockSpec((tm, WIDTH), row), pl.BlockSpec((tm, WIDTH), row),
                  pl.BlockSpec((tm, 2 * d), row),
                  pl.BlockSpec((WIDTH, d), const), pl.BlockSpec((WIDTH, d), const), pl.BlockSpec((d, d), const),
                  pl.BlockSpec((1, d), const), pl.BlockSpec((1, d), const),
                  pl.BlockSpec((d, LANES), const), pl.BlockSpec((1, LANES), const)],
        out_specs=[pl.BlockSpec((tm, d), row), pl.BlockSpec((tm, LANES), row)],
        out_shape=[jax.ShapeDtypeStruct((t, d), F32), jax.ShapeDtypeStruct((t, LANES), F32)],
        compiler_params=pltpu.CompilerParams(dimension_semantics=("arbitrary",), vmem_limit_bytes=VMEM_LIMIT),
        name=name,
    )(x, o_a, o_b, gates, wpa, wpb, wo, ln_g, ln_b, w_router, b_router)


def _moe_kernel(h_ref, comb_ref, wgu_ref, wd_ref, lng_ref, lnb_ref, o_ref, hb_scr, acc_scr, *, alpha):
    e = pl.program_id(1)
    f = wd_ref.shape[0]

    @pl.when(e == 0)
    def _():
        hb_scr[...] = h_ref[...].astype(BF16)
        acc_scr[...] = jnp.zeros(acc_scr.shape, F32)

    gu = _dot(hb_scr[...], wgu_ref[...])
    gate, up = gu[:, :f], gu[:, f:]
    comb = comb_ref[...]
    lane = lax.broadcasted_iota(jnp.int32, comb.shape, 1)
    c = jnp.sum(jnp.where(lane == e, comb, 0.0), axis=1, keepdims=True)
    act = gate / (1.0 + jnp.exp(-gate)) * up * c
    acc_scr[...] += _dot(act.astype(BF16), wd_ref[...])

    @pl.when(e == pl.num_programs(1) - 1)
    def _():
        o_ref[...] = _layer_norm(alpha * h_ref[...] + acc_scr[...], lng_ref[...], lnb_ref[...])


def _moe(h, comb, wgu, wd, ln_g, ln_b, *, alpha, name):
    t, d = h.shape
    tm = next(c for c in (1024, 512, 256) if t % c == 0)
    n_e, _, f2 = wgu.shape
    row = lambda i, e: (i, 0)
    const = lambda i, e: (0, 0)
    return pl.pallas_call(
        functools.partial(_moe_kernel, alpha=alpha),
        grid=(t // tm, n_e),
        in_specs=[pl.BlockSpec((tm, d), row), pl.BlockSpec((tm, LANES), row),
                  pl.BlockSpec((None, d, f2), lambda i, e: (e, 0, 0)),
                  pl.BlockSpec((None, f2 // 2, d), lambda i, e: (e, 0, 0)),
                  pl.BlockSpec((1, d), const), pl.BlockSpec((1, d), const)],
        out_specs=pl.BlockSpec((tm, d), row),
        out_shape=jax.ShapeDtypeStruct((t, d), F32),
        scratch_shapes=[pltpu.VMEM((tm, d), BF16), pltpu.VMEM((tm, d), F32)],
        compiler_params=pltpu.CompilerParams(dimension_semantics=("arbitrary", "arbitrary"),
                                             vmem_limit_bytes=VMEM_LIMIT),
        name=name,
    )(h, comb, wgu, wd, ln_g, ln_b)


def kernel(x_prompt, x_sample, cache_moba_k, cache_moba_v, cache_diff_k, cache_diff_v, page_table, w_in, b_gate, lambda_q1, lambda_k1, lambda_q2, lambda_k2, subln_w, w_proj_a, w_proj_b, w_out, ln1_g, ln1_b, ln2_g, ln2_b, w_route_group, b_route_group, w_route_expert, b_route_expert, w_expert_gate, w_expert_up, w_expert_down):
    depth = w_in.shape[0]
    bsz, seq, d = x_prompt.shape
    n_seq, n_new, _ = x_sample.shape
    alpha = (2 * depth) ** 0.25
    slopes_a = _alibi_slopes(A_HEADS) * LOG2E
    slopes_b = _alibi_slopes(B_HEADS) * LOG2E
    slopes_a_smem = jnp.asarray(slopes_a, F32)
    slopes_b_smem = jnp.asarray(slopes_b, F32)
    h_p = x_prompt.reshape(bsz * seq, d)
    h_s = x_sample.reshape(n_seq * n_new, d)
    rows_p, rows_s = [], []
    for l in range(depth):
        lam_init = _lambda_init(l)
        lam = (jnp.exp(jnp.sum(lambda_q1[l].astype(F32) * lambda_k1[l].astype(F32)))
               - jnp.exp(jnp.sum(lambda_q2[l].astype(F32) * lambda_k2[l].astype(F32))) + lam_init).reshape(1)
        w_bf = w_in[l].astype(BF16)
        w_t = [w_bf[:, slot * WIDTH:(slot + 1) * WIDTH].T for slot in (1, 2, 5)]
        subw = subln_w[l].reshape(1, B_V_DIM)
        wpa, wpb, wo = w_proj_a[l].astype(BF16), w_proj_b[l].astype(BF16), w_out[l].astype(BF16)
        n_route = N_EXPERTS + N_GROUPS
        w_router = jnp.pad(jnp.concatenate([w_route_expert[l], w_route_group[l]], axis=1),
                           ((0, 0), (0, LANES - n_route))).astype(BF16)
        b_router = jnp.pad(jnp.concatenate([b_route_expert[l], b_route_group[l]]), (0, LANES - n_route)).reshape(1, LANES)
        wgu = jnp.concatenate([w_expert_gate[l], w_expert_up[l]], axis=2).astype(BF16)
        wd = w_expert_down[l].astype(BF16)
        ln1 = (ln1_g[l].reshape(1, d), ln1_b[l].reshape(1, d))
        ln2 = (ln2_g[l].reshape(1, d), ln2_b[l].reshape(1, d))

        (qa, qb, kb, vb, gates, ka_bf, kb_bf, kat, vat, vat_bf, vbt_bf, kmean) = _in_projection(
            h_p, w_bf, b_gate[l], w_t, prompt=True, bsz=bsz)
        o_a = _moba_prompt(qa, ka_bf, vat_bf, kmean.reshape(bsz, seq // MOBA_BLOCK, WIDTH), slopes_a_smem,
                           bsz=bsz, seq=seq)
        o_b = _diff_prompt(qb, kb_bf, vbt_bf, slopes_b_smem, lam, subw, bsz=bsz, seq=seq, lam_init=lam_init)
        h1, comb = _merge(h_p, o_a, o_b, gates, wpa, wpb, wo, *ln1, w_router, b_router, alpha=alpha,
                          name="merge_prompt")
        h_p = _moe(h1, comb, wgu, wd, *ln2, alpha=alpha, name="moe_prompt")
        from_feat = lambda xt: jnp.transpose(xt.reshape(bsz, A_HEADS, A_HEAD_DIM, seq), (0, 3, 1, 2))
        rows_p.append((from_feat(kat), from_feat(vat),
                       kb.reshape(bsz, seq, B_HEADS, 2 * B_HEAD_DIM), vb.reshape(bsz, seq, B_HEADS, B_V_DIM)))

        qa, ka, va, qb, kb, vb, gates = _in_projection(h_s, w_bf, b_gate[l], None, prompt=False)
        o_a, o_b = _sample_attention(qa, ka, va, qb, kb, vb, cache_moba_k[l], cache_moba_v[l], cache_diff_k[l],
                                     cache_diff_v[l], page_table, slopes_a, slopes_b, lam, subw, lam_init=lam_init)
        h1, comb = _merge(h_s, o_a, o_b, gates, wpa, wpb, wo, *ln1, w_router, b_router, alpha=alpha,
                          name="merge_sample")
        h_s = _moe(h1, comb, wgu, wd, *ln2, alpha=alpha, name="moe_sample")
        rows_s.append((ka.reshape(n_seq, n_new, A_HEADS, A_HEAD_DIM), va.reshape(n_seq, n_new, A_HEADS, A_HEAD_DIM),
                       kb.reshape(n_seq, n_new, B_HEADS, 2 * B_HEAD_DIM), vb.reshape(n_seq, n_new, B_HEADS, B_V_DIM)))

    stack = lambda rows, k: jnp.stack([r[k] for r in rows])
    return (h_p.reshape(bsz, seq, d), h_s.reshape(n_seq, n_new, d),
            stack(rows_p, 0), stack(rows_p, 1), stack(rows_p, 2), stack(rows_p, 3),
            stack(rows_s, 0), stack(rows_s, 1), stack(rows_s, 2), stack(rows_s, 3))
```
